```python
import math
import jax, jax.numpy as jnp
from jax import lax
import numpy as np

D_MODEL = 2048
BATCH = 1
SEQ = 8192
DEPTH = 1
DEC_BATCH = 32
DEC_SEQ = 8
PAST_LEN = 16384
PAGE_SIZE = 128

N_META = 16
N_HEADS = 16
N_KV_HEADS = 4
HEAD_DIM = 128
GROUP = N_HEADS // N_KV_HEADS
ATTN_W = N_HEADS * HEAD_DIM
KV_W = N_KV_HEADS * HEAD_DIM
CONV_CH = D_MODEL
CONV_WIDTH = 31
N_IDX_HEADS = 16
IDX_DIM = 128
MAX_TOPK = 256
D_FF = 5632
ROPE_THETA = 10000.0
Q_BLOCK = 128
EPS = 1e-6
SPLITS = (CONV_CH, CONV_CH, ATTN_W, KV_W, KV_W, N_IDX_HEADS * IDX_DIM, IDX_DIM, N_IDX_HEADS, D_MODEL, D_MODEL)
IN_W = sum(SPLITS)

kernel_name = 'macaron_conv_dsa_gated_hybrid_step'


def rmsnorm(x, g):
    xf = x.astype(jnp.float32)
    y = xf * lax.rsqrt(jnp.mean(xf * xf, -1, keepdims=True) + EPS)
    return (y * g.astype(jnp.float32)).astype(x.dtype)


def layernorm(x, g, b):
    xf = x.astype(jnp.float32)
    mu = jnp.mean(xf, -1, keepdims=True)
    var = jnp.mean(jnp.square(xf - mu), -1, keepdims=True)
    return ((xf - mu) * lax.rsqrt(var + EPS) * g.astype(jnp.float32) + b.astype(jnp.float32)).astype(x.dtype)


def swiglu(x, w1, w3, w2):
    return (jax.nn.silu(x @ w1) * (x @ w3)) @ w2


def rope(x, pos):
    half = x.shape[-1] // 2
    inv = ROPE_THETA ** (-jnp.arange(half, dtype=jnp.float32) / half)
    ang = pos.astype(jnp.float32)[:, None] * inv[None, :]
    cos = jnp.cos(ang)[None, :, None, :]
    sin = jnp.sin(ang)[None, :, None, :]
    xf = x.astype(jnp.float32)
    x1, x2 = xf[..., :half], xf[..., half:]
    return jnp.concatenate([x1 * cos - x2 * sin, x1 * sin + x2 * cos], -1).astype(x.dtype)


def split_cols(z):
    points = [int(i) for i in np.cumsum(SPLITS)[:-1]]
    return jnp.split(z, points, axis=-1)


def pre_mix(h, pos, p):
    h = h + 0.5 * swiglu(rmsnorm(h, p['norm1_g']), p['ffn1_w1'], p['ffn1_w3'], p['ffn1_w2'])
    u = rmsnorm(h, p['norm_mix_g'])
    B, T = u.shape[:2]
    glu_a, glu_b, q, k, v, iq, ik, iw, g_conv, g_attn = split_cols(u @ p['w_in'])
    glu = glu_a * jax.nn.sigmoid(glu_b)
    q = rope(q.reshape(B, T, N_HEADS, HEAD_DIM), pos)
    k = rope(k.reshape(B, T, N_KV_HEADS, HEAD_DIM), pos)
    v = v.reshape(B, T, N_KV_HEADS, HEAD_DIM)
    iq = rope(iq.reshape(B, T, N_IDX_HEADS, IDX_DIM), pos)
    ik = rope(ik[:, :, None, :], pos)[:, :, 0]
    return h, (glu, q, k, v, iq, ik, iw, g_conv, g_attn)


def conv_branch(glu_ext, p):
    c = lax.conv_general_dilated(glu_ext, p['w_dw'][:, None, :], window_strides=(1,), padding='VALID',
                                 dimension_numbers=('NWC', 'WIO', 'NWC'), feature_group_count=CONV_CH)
    c = jax.nn.silu(layernorm(c + p['b_dw'], p['conv_ln_g'], p['conv_ln_b']))
    return c @ p['w_conv_out'] + p['b_conv_out']


def post_mix(h, y_conv, attn_o, g_conv, g_attn, p):
    y_attn = attn_o @ p['w_o']
    m = jax.nn.sigmoid(g_conv) * y_conv + jax.nn.sigmoid(g_attn) * y_attn
    h = h + m @ p['w_out']
    return h + 0.5 * swiglu(rmsnorm(h, p['norm2_g']), p['ffn2_w1'], p['ffn2_w3'], p['ffn2_w2'])


def index_scores(iq, iw, ik):
    s = jnp.einsum('bqhd,bkd->bqhk', iq, ik, preferred_element_type=jnp.float32) * (IDX_DIM ** -0.5)
    return jnp.einsum('bqhk,bqh->bqk', jax.nn.relu(s), iw.astype(jnp.float32)) * (N_IDX_HEADS ** -0.5)


def sparse_attend(q, k_sel, v_sel, valid):
    B, Tq = q.shape[:2]
    qg = q.reshape(B, Tq, N_KV_HEADS, GROUP, HEAD_DIM)
    logits = jnp.einsum('btngd,btsnd->btngs', qg, k_sel, preferred_element_type=jnp.float32) * (HEAD_DIM ** -0.5)
    logits = jnp.where(valid[:, :, None, None, :], logits, -jnp.inf)
    probs = jax.nn.softmax(logits, axis=-1)
    o = jnp.einsum('btngs,btsnd->btngd', probs.astype(v_sel.dtype), v_sel)
    return o.reshape(B, Tq, ATTN_W)


def prompt_attention(q, iq, iw, k, v, ik, topk):
    B, T = q.shape[:2]
    n_blk = -(-T // Q_BLOCK)
    pad = n_blk * Q_BLOCK - T

    def blocks(a):
        a = jnp.pad(a, [(0, 0), (0, pad)] + [(0, 0)] * (a.ndim - 2))
        return jnp.moveaxis(a.reshape((B, n_blk, Q_BLOCK) + a.shape[2:]), 1, 0)

    key_pos = jnp.arange(T, dtype=jnp.int32)
    take = jax.vmap(lambda a, i: a[i])

    def one_block(args):
        qb, iqb, iwb, start = args
        qpos = start + jnp.arange(Q_BLOCK, dtype=jnp.int32)
        vis = key_pos[None, :] <= qpos[:, None]
        sc = jnp.where(vis[None], index_scores(iqb, iwb, ik), -jnp.inf)
        _, sel = lax.top_k(sc, topk)
        valid = sel <= qpos[None, :, None]
        return sparse_attend(qb, take(k, sel), take(v, sel), valid)

    starts = jnp.arange(n_blk, dtype=jnp.int32) * Q_BLOCK
    out = lax.map(one_block, (blocks(q), blocks(iq), blocks(iw), starts))
    return jnp.moveaxis(out, 0, 1).reshape(B, n_blk * Q_BLOCK, ATTN_W)[:, :T]


def sample_attention(layer, q, iq, iw, k_new, v_new, ik_new, cache_k, cache_v, cache_ik, page_table, topk):
    DB, DS = q.shape[:2]
    past = page_table.shape[1] * PAGE_SIZE
    ik_past = cache_ik[layer, page_table].reshape(DB, past, IDX_DIM)
    ik_all = jnp.concatenate([ik_past, ik_new], axis=1)
    L = past + DS
    qpos = past + jnp.arange(DS, dtype=jnp.int32)
    vis = jnp.arange(L, dtype=jnp.int32)[None, :] <= qpos[:, None]
    sc = jnp.where(vis[None], index_scores(iq, iw, ik_all), -jnp.inf)
    _, sel = lax.top_k(sc, topk)
    valid = sel <= qpos[None, :, None]
    in_past = (sel < past)[..., None, None]
    ps = jnp.minimum(sel, past - 1)
    phys = jax.vmap(lambda pt, pg: pt[pg])(page_table, ps // PAGE_SIZE)
    off = ps % PAGE_SIZE
    ns = jnp.clip(sel - past, 0, DS - 1)
    take = jax.vmap(lambda a, i: a[i])
    k_sel = jnp.where(in_past, cache_k[layer, phys, off], take(k_new, ns))
    v_sel = jnp.where(in_past, cache_v[layer, phys, off], take(v_new, ns))
    return sparse_attend(q, k_sel, v_sel, valid)


def setup_inputs(seed: int = 0) -> dict:
    key = jax.random.key(seed)
    ks = iter(jax.random.split(key, 40))
    nrm = lambda shape, s=1.0: jax.random.normal(next(ks), shape, jnp.float32) * s
    n_pages = PAST_LEN // PAGE_SIZE
    n_pool = (DEC_BATCH * n_pages * 5) // 4
    page_table = jax.random.permutation(next(ks), n_pool)[:DEC_BATCH * n_pages].reshape(DEC_BATCH, n_pages).astype(jnp.int32)
    gain = lambda: 1.0 + nrm((DEPTH, D_MODEL), 0.02)
    return {
        'x_prompt': nrm((BATCH, SEQ, D_MODEL)),
        'x_sample': nrm((DEC_BATCH, DEC_SEQ, D_MODEL)),
        'cache_k': nrm((DEPTH, n_pool, PAGE_SIZE, N_KV_HEADS, HEAD_DIM)),
        'cache_v': nrm((DEPTH, n_pool, PAGE_SIZE, N_KV_HEADS, HEAD_DIM)),
        'cache_ik': nrm((DEPTH, n_pool, PAGE_SIZE, IDX_DIM)),
        'state_conv': nrm((DEPTH, DEC_BATCH, CONV_WIDTH - 1, CONV_CH), 0.5),
        'page_table': page_table,
        'meta': nrm((N_META, D_MODEL)),
        'norm1_g': gain(),
        'ffn1_w1': nrm((DEPTH, D_MODEL, D_FF), D_MODEL ** -0.5),
        'ffn1_w3': nrm((DEPTH, D_MODEL, D_FF), D_MODEL ** -0.5),
        'ffn1_w2': nrm((DEPTH, D_FF, D_MODEL), D_FF ** -0.5),
        'norm_mix_g': gain(),
        'w_in': nrm((DEPTH, D_MODEL, IN_W), D_MODEL ** -0.5),
        'w_dw': nrm((DEPTH, CONV_WIDTH, CONV_CH), CONV_WIDTH ** -0.5),
        'b_dw': nrm((DEPTH, CONV_CH), 0.02),
        'conv_ln_g': 1.0 + nrm((DEPTH, CONV_CH), 0.02),
        'conv_ln_b': nrm((DEPTH, CONV_CH), 0.02),
        'w_conv_out': nrm((DEPTH, CONV_CH, D_MODEL), CONV_CH ** -0.5),
        'b_conv_out': nrm((DEPTH, D_MODEL), 0.02),
        'w_o': nrm((DEPTH, ATTN_W, D_MODEL), ATTN_W ** -0.5),
        'w_out': nrm((DEPTH, D_MODEL, D_MODEL), D_MODEL ** -0.5),
        'norm2_g': gain(),
        'ffn2_w1': nrm((DEPTH, D_MODEL, D_FF), D_MODEL ** -0.5),
        'ffn2_w3': nrm((DEPTH, D_MODEL, D_FF), D_MODEL ** -0.5),
        'ffn2_w2': nrm((DEPTH, D_FF, D_MODEL), D_FF ** -0.5),
        'final_g': 1.0 + nrm((D_MODEL,), 0.02),
    }


def reference(x_prompt, x_sample, cache_k, cache_v, cache_ik, state_conv, page_table, meta,
              norm1_g, ffn1_w1, ffn1_w3, ffn1_w2, norm_mix_g, w_in, w_dw, b_dw, conv_ln_g, conv_ln_b,
              w_conv_out, b_conv_out, w_o, w_out, norm2_g, ffn2_w1, ffn2_w3, ffn2_w2, final_g):
    B, S, _ = x_prompt.shape
    DB, DS, _ = x_sample.shape
    past = page_table.shape[1] * PAGE_SIZE
    topk_p = min(MAX_TOPK, S // 4)
    topk_s = min(MAX_TOPK, (past + DS) // 4)

    hp = jnp.concatenate([jnp.broadcast_to(meta[None], (B, N_META, D_MODEL)), x_prompt], axis=1)
    pos_p = jnp.arange(N_META + S, dtype=jnp.int32)
    hs = x_sample
    pos_s = past + jnp.arange(DS, dtype=jnp.int32)

    kp, vp, ikp, cp, kss, vss, iks, css = [], [], [], [], [], [], [], []
    for l in range(DEPTH):
        p = dict(norm1_g=norm1_g[l], ffn1_w1=ffn1_w1[l], ffn1_w3=ffn1_w3[l], ffn1_w2=ffn1_w2[l],
                 norm_mix_g=norm_mix_g[l], w_in=w_in[l], w_dw=w_dw[l], b_dw=b_dw[l],
                 conv_ln_g=conv_ln_g[l], conv_ln_b=conv_ln_b[l], w_conv_out=w_conv_out[l],
                 b_conv_out=b_conv_out[l], w_o=w_o[l], w_out=w_out[l], norm2_g=norm2_g[l],
                 ffn2_w1=ffn2_w1[l], ffn2_w3=ffn2_w3[l], ffn2_w2=ffn2_w2[l])
        hp, (glu, q, k, v, iq, ik, iw, gc, ga) = pre_mix(hp, pos_p, p)
        glu_ext = jnp.pad(glu, ((0, 0), (CONV_WIDTH - 1, 0), (0, 0)))
        y_conv = conv_branch(glu_ext, p)
        attn_o = prompt_attention(q, iq, iw, k, v, ik, topk_p)
        hp = post_mix(hp, y_conv, attn_o, gc, ga, p)
        kp.append(k); vp.append(v); ikp.append(ik); cp.append(glu_ext[:, -(CONV_WIDTH - 1):])
        hs, (glu, q, k, v, iq, ik, iw, gc, ga) = pre_mix(hs, pos_s, p)
        glu_ext = jnp.concatenate([state_conv[l], glu], axis=1)
        y_conv = conv_branch(glu_ext, p)
        attn_o = sample_attention(l, q, iq, iw, k, v, ik, cache_k, cache_v, cache_ik, page_table, topk_s)
        hs = post_mix(hs, y_conv, attn_o, gc, ga, p)
        kss.append(k); vss.append(v); iks.append(ik); css.append(glu_ext[:, -(CONV_WIDTH - 1):])

    y_prompt = rmsnorm(hp, final_g)[:, N_META:]
    y_sample = rmsnorm(hs, final_g)
    return (y_prompt, y_sample,
            jnp.stack(kp), jnp.stack(vp), jnp.stack(ikp), jnp.stack(cp),
            jnp.stack(kss), jnp.stack(vss), jnp.stack(iks), jnp.stack(css))
```

```python
import functools

import numpy as np
import jax
import jax.numpy as jnp
from jax import lax
from jax.experimental import pallas as pl
from jax.experimental.pallas import tpu as pltpu

MAX_TOPK = 256
ROPE_THETA = 10000.0
EPS = 1e-6

LANES = 128
ROW_TILE = 512
Q_TILE = 256
FF_TILE = 512
PAGES_PER_STEP = 8
VMEM_LIMIT = 56 * 1024 * 1024
INT_MIN = -(2 ** 31)
NEG = -1e30

_bf16 = jnp.bfloat16
_f32 = jnp.float32


def _cparams(sem):
    return pltpu.CompilerParams(dimension_semantics=sem, vmem_limit_bytes=VMEM_LIMIT)


def _dot(a, b):
    return jnp.dot(a, b, preferred_element_type=_f32)


def _dot_nt(a, b):
    return lax.dot_general(a, b, (((1,), (1,)), ((), ())), preferred_element_type=_f32)


def _rms(x, g):
    return x * lax.rsqrt(jnp.mean(x * x, axis=-1, keepdims=True) + EPS) * g


def _sigmoid(x):
    return 1.0 / (1.0 + jnp.exp(-x))


def _ffn_kernel(h_ref, g_ref, w1_ref, w3_ref, w2_ref, g2_ref, *refs, n_f, emit_h):
    if emit_h:
        h_out_ref, u_out_ref, u_sc, acc_sc = refs
    else:
        y_out_ref, u_sc, acc_sc = refs
    f = pl.program_id(1)

    @pl.when(f == 0)
    def _():
        u_sc[...] = _rms(h_ref[...], g_ref[...]).astype(_bf16)
        acc_sc[...] = jnp.zeros_like(acc_sc)

    u = u_sc[...]
    a = _dot(u, w1_ref[...])
    b = _dot(u, w3_ref[...])
    gate = (a * _sigmoid(a)) * b
    acc_sc[...] += _dot(gate.astype(_bf16), w2_ref[...])

    @pl.when(f == n_f - 1)
    def _():
        hn = h_ref[...] + 0.5 * acc_sc[...]
        un = _rms(hn, g2_ref[...])
        if emit_h:
            h_out_ref[...] = hn
            u_out_ref[...] = un.astype(_bf16)
        else:
            y_out_ref[...] = un


def _ffn(h, g, w1, w3, w2, g2, *, emit_h):
    mp, d = h.shape
    dff = w1.shape[1]
    tf = min(FF_TILE, dff)
    assert mp % ROW_TILE == 0 and dff % tf == 0
    n_f = dff // tf
    row = pl.BlockSpec((ROW_TILE, d), lambda i, f: (i, 0))
    vec = pl.BlockSpec((1, d), lambda i, f: (0, 0))
    if emit_h:
        out_shape = (jax.ShapeDtypeStruct((mp, d), _f32), jax.ShapeDtypeStruct((mp, d), _bf16))
        out_specs = (row, row)
    else:
        out_shape = jax.ShapeDtypeStruct((mp, d), _f32)
        out_specs = row
    return pl.pallas_call(
        functools.partial(_ffn_kernel, n_f=n_f, emit_h=emit_h),
        grid=(mp // ROW_TILE, n_f),
        in_specs=[row, vec,
                  pl.BlockSpec((d, tf), lambda i, f: (0, f)),
                  pl.BlockSpec((d, tf), lambda i, f: (0, f)),
                  pl.BlockSpec((tf, d), lambda i, f: (f, 0)),
                  vec],
        out_specs=out_specs,
        out_shape=out_shape,
        scratch_shapes=[pltpu.VMEM((ROW_TILE, d), _bf16), pltpu.VMEM((ROW_TILE, d), _f32)],
        compiler_params=_cparams(("parallel", "arbitrary")),
        name="ffn_emit_h" if emit_h else "ffn_final",
    )(h, g, w1, w3, w2, g2)


def _glu_kernel(u_ref, wa_ref, wb_ref, o_ref):
    u = u_ref[...]
    o_ref[...] = _dot(u, wa_ref[...]) * _sigmoid(_dot(u, wb_ref[...]))


def _proj_glu(u, wa, wb):
    mp, d = u.shape
    n = wa.shape[1]
    tn = min(512, n)
    assert n % tn == 0
    return pl.pallas_call(
        _glu_kernel,
        grid=(mp // ROW_TILE, n // tn),
        in_specs=[pl.BlockSpec((ROW_TILE, d), lambda i, j: (i, 0)),
                  pl.BlockSpec((d, tn), lambda i, j: (0, j)),
                  pl.BlockSpec((d, tn), lambda i, j: (0, j))],
        out_specs=pl.BlockSpec((ROW_TILE, tn), lambda i, j: (i, j)),
        out_shape=jax.ShapeDtypeStruct((mp, n), _f32),
        compiler_params=_cparams(("parallel", "parallel")),
        name="proj_glu",
    )(u, wa, wb)


def _proj_kernel(u_ref, w_ref, *refs, mode, scale, out_dtypes):
    if mode == "rope":
        cos_ref, sin_ref = refs[:2]
        outs = refs[2:]
    else:
        outs = refs
    z = _dot(u_ref[...], w_ref[...])
    if mode == "sigmoid":
        z = _sigmoid(z)
    elif mode == "rope":
        cos = cos_ref[...]
        sin = sin_ref[...]
        parts = []
        for j in range(z.shape[1] // LANES):
            x = z[:, j * LANES:(j + 1) * LANES]
            parts.append(x * cos + pltpu.roll(x, LANES // 2, 1) * sin)
        z = parts[0] if len(parts) == 1 else jnp.concatenate(parts, axis=1)
    if scale != 1.0:
        z = z * scale
    for o_ref, dt in zip(outs, out_dtypes):
        o_ref[...] = z.astype(dt)


def _proj(u, w, *, mode="plain", cos=None, sin=None, scale=1.0, out_dtypes=(_f32,), name="proj"):
    mp, d = u.shape
    n = w.shape[1]
    tn = n if n <= 512 else 512
    assert n % tn == 0
    in_specs = [pl.BlockSpec((ROW_TILE, d), lambda i, j: (i, 0)),
                pl.BlockSpec((d, tn), lambda i, j: (0, j))]
    args = [u, w]
    if mode == "rope":
        in_specs += [pl.BlockSpec((ROW_TILE, LANES), lambda i, j: (i, 0))] * 2
        args += [cos, sin]
    ospec = pl.BlockSpec((ROW_TILE, tn), lambda i, j: (i, j))
    outs = pl.pallas_call(
        functools.partial(_proj_kernel, mode=mode, scale=scale, out_dtypes=out_dtypes),
        grid=(mp // ROW_TILE, n // tn),
        in_specs=in_specs,
        out_specs=tuple(ospec for _ in out_dtypes),
        out_shape=tuple(jax.ShapeDtypeStruct((mp, n), dt) for dt in out_dtypes),
        compiler_params=_cparams(("parallel", "parallel")),
        name=name,
    )(*args)
    return outs


def _ln_swish(c, g, b):
    mu = jnp.mean(c, axis=-1, keepdims=True)
    var = jnp.mean(jnp.square(c - mu), axis=-1, keepdims=True)
    y = (c - mu) * lax.rsqrt(var + EPS) * g + b
    return y * _sigmoid(y)


def _conv_prompt_kernel(cur_ref, halo_ref, w_ref, bdw_ref, g_ref, b_ref, o_ref, ext_sc, sh_sc, c_sc,
                        *, width, tc, halo, cblk):
    i = pl.program_id(0)
    ch = cur_ref.shape[1]
    ext_sc[pl.ds(0, halo), :] = jnp.where(i > 0, halo_ref[...], 0.0)
    ext_sc[pl.ds(halo, tc), :] = cur_ref[...]
    ext_sc[pl.ds(halo + tc, 8), :] = jnp.zeros((8, ch), _f32)
    lead = halo - (width - 1)
    for cb in range(ch // cblk):
        cs = pl.ds(cb * cblk, cblk)
        for r in range(8):
            sh_sc[r] = ext_sc[pl.ds(r, halo + tc), cs]

        def rows(t, _):
            t0 = pl.multiple_of(t * 8, 8)
            acc = jnp.broadcast_to(bdw_ref[:, cs], (8, cblk))
            for j in range(width):
                off = lead + j
                x = sh_sc[off % 8, pl.ds(t0 + (off // 8) * 8, 8), :]
                acc = acc + x * w_ref[pl.ds(j, 1), cs]
            c_sc[pl.ds(t0, 8), cs] = acc
            return 0

        lax.fori_loop(0, tc // 8, rows, 0)
    o_ref[...] = _ln_swish(c_sc[...], g_ref[...], b_ref[...]).astype(_bf16)


def _conv_prompt(glu, tp, w_dw, b_dw, ln_g, ln_b):
    ch = glu.shape[1]
    width = w_dw.shape[0]
    tc = Q_TILE
    halo = 32
    assert width - 1 <= halo and tp % tc == 0 and tc % halo == 0
    cblk = min(512, ch)
    vec = pl.BlockSpec((1, ch), lambda i: (0, 0))
    return pl.pallas_call(
        functools.partial(_conv_prompt_kernel, width=width, tc=tc, halo=halo, cblk=cblk),
        grid=(tp // tc,),
        in_specs=[pl.BlockSpec((tc, ch), lambda i: (i, 0)),
                  pl.BlockSpec((halo, ch), lambda i: (jnp.maximum(i * (tc // halo) - 1, 0), 0)),
                  pl.BlockSpec((width, ch), lambda i: (0, 0)),
                  vec, vec, vec],
        out_specs=pl.BlockSpec((tc, ch), lambda i: (i, 0)),
        out_shape=jax.ShapeDtypeStruct((tp, ch), _bf16),
        scratch_shapes=[pltpu.VMEM((halo + tc + 8, ch), _f32),
                        pltpu.VMEM((8, halo + tc, cblk), _f32),
                        pltpu.VMEM((tc, ch), _f32)],
        compiler_params=_cparams(("parallel",)),
        name="conv_prompt",
    )(glu, glu, w_dw, b_dw, ln_g, ln_b)


def _conv_sample_kernel(ext_ref, w_ref, bdw_ref, g_ref, b_ref, o_ref, *, width, ds):
    ch = ext_ref.shape[2]
    acc = jnp.broadcast_to(bdw_ref[...], (ds, ch))
    for j in range(width):
        acc = acc + ext_ref[0, pl.ds(j, ds), :] * w_ref[pl.ds(j, 1), :]
    o_ref[0] = _ln_swish(acc, g_ref[...], b_ref[...]).astype(_bf16)


def _conv_sample(ext, w_dw, b_dw, ln_g, ln_b, ds):
    db, rows, ch = ext.shape
    width = w_dw.shape[0]
    vec = pl.BlockSpec((1, ch), lambda b: (0, 0))
    return pl.pallas_call(
        functools.partial(_conv_sample_kernel, width=width, ds=ds),
        grid=(db,),
        in_specs=[pl.BlockSpec((1, rows, ch), lambda b: (b, 0, 0)),
                  pl.BlockSpec((width, ch), lambda b: (0, 0)),
                  vec, vec, vec],
        out_specs=pl.BlockSpec((1, ds, ch), lambda b: (b, 0, 0)),
        out_shape=jax.ShapeDtypeStruct((db, ds, ch), _bf16),
        compiler_params=_cparams(("parallel",)),
        name="conv_sample",
    )(ext, w_dw, b_dw, ln_g, ln_b)


def _merge_kernel(ca_ref, ao_ref, wco_ref, wo_ref, bco_ref, sgc_ref, sga_ref, m_ref):
    yc = _dot(ca_ref[...], wco_ref[...]) + bco_ref[...]
    ya = _dot(ao_ref[...], wo_ref[...])
    m_ref[...] = (sgc_ref[...] * yc + sga_ref[...] * ya).astype(_bf16)


def _merge(ca, ao, wco, wo, bco, sg):
    mp, ch = ca.shape
    aw = ao.shape[1]
    d = wco.shape[1]
    tn = min(512, d)
    nj = d // tn
    return pl.pallas_call(
        _merge_kernel,
        grid=(mp // ROW_TILE, nj),
        in_specs=[pl.BlockSpec((ROW_TILE, ch), lambda i, j: (i, 0)),
                  pl.BlockSpec((ROW_TILE, aw), lambda i, j: (i, 0)),
                  pl.BlockSpec((ch, tn), lambda i, j: (0, j)),
                  pl.BlockSpec((aw, tn), lambda i, j: (0, j)),
                  pl.BlockSpec((1, tn), lambda i, j: (0, j)),
                  pl.BlockSpec((ROW_TILE, tn), lambda i, j: (i, j)),
                  pl.BlockSpec((ROW_TILE, tn), lambda i, j: (i, j + nj))],
        out_specs=pl.BlockSpec((ROW_TILE, tn), lambda i, j: (i, j)),
        out_shape=jax.ShapeDtypeStruct((mp, d), _bf16),
        compiler_params=_cparams(("parallel", "parallel")),
        name="merge",
    )(ca, ao, wco, wo, bco, sg, sg)


def _resid_kernel(m_ref, w_ref, h_ref, o_ref):
    o_ref[...] = h_ref[...] + _dot(m_ref[...], w_ref[...])


def _resid_proj(m, w, h):
    mp, d = m.shape
    n = w.shape[1]
    tn = min(512, n)
    return pl.pallas_call(
        _resid_kernel,
        grid=(mp // ROW_TILE, n // tn),
        in_specs=[pl.BlockSpec((ROW_TILE, d), lambda i, j: (i, 0)),
                  pl.BlockSpec((d, tn), lambda i, j: (0, j)),
                  pl.BlockSpec((ROW_TILE, tn), lambda i, j: (i, j))],
        out_specs=pl.BlockSpec((ROW_TILE, tn), lambda i, j: (i, j)),
        out_shape=jax.ShapeDtypeStruct((mp, n), _f32),
        compiler_params=_cparams(("parallel", "parallel")),
        name="resid_proj",
    )(m, w, h)


def _order_key(s):
    b = lax.bitcast_convert_type(s, jnp.int32)
    return b ^ (lax.shift_right_arithmetic(b, 31) & jnp.int32(0x7FFFFFFF))


def _bit_search(count_ge, rows, n_bits, k, signed):
    flip = jnp.int32(INT_MIN) if signed else jnp.int32(0)

    def body(b, t_u):
        bit = lax.shift_left(jnp.int32(1), (n_bits - 1 - b).astype(jnp.int32))
        cand = t_u | bit
        cnt = count_ge(cand ^ flip)
        return jnp.where(cnt >= k, cand, t_u)

    t_u = lax.fori_loop(0, n_bits, body, jnp.zeros((rows, 1), jnp.int32))
    return t_u ^ flip


def _select_threshold(load, store, n_chunks, rows, width, k, idx_bits):
    halves = width // LANES

    def count(pred):
        def body(c, acc):
            m = pred(load(c), c)
            for j in range(halves):
                acc = acc + jnp.where(m[:, j * LANES:(j + 1) * LANES], 1.0, 0.0)
            return acc
        acc = lax.fori_loop(0, n_chunks, body, jnp.zeros((rows, LANES), _f32))
        return jnp.sum(acc, axis=1, keepdims=True)

    kf = jnp.float32(k)
    t = _bit_search(lambda th: count(lambda x, c: x >= th), rows, 32, kf, True)
    t = jnp.maximum(t, jnp.int32(INT_MIN + 1))
    n_ge = count(lambda x, c: x >= t)

    @pl.when(jnp.max(n_ge) > kf)
    def _():
        n_gt = count(lambda x, c: x > t)
        need = kf - n_gt
        lane = lax.broadcasted_iota(jnp.int32, (rows, width), 1)

        def below(x, c, j):
            return (x == t) & ((lane + c * width) < j)

        def count_ge_fn(j):
            return need - count(lambda x, c: below(x, c, j)) + (kf - 1.0)
        j_cut = _bit_search(count_ge_fn, rows, idx_bits, kf, False)
        surplus = n_ge > kf

        def rewrite(c, _):
            x = load(c)
            drop = (x == t) & ((lane + c * width) > j_cut) & surplus
            store(c, jnp.where(drop, jnp.int32(INT_MIN), x))
            return 0
        lax.fori_loop(0, n_chunks, rewrite, 0)

    return t


def _prompt_attn_kernel(iq_ref, iw_ref, q_ref, ikt_ref, kt_ref, v_ref, o_ref,
                        key_sc, thr_sc, wb_sc, m_sc, l_sc, acc_sc, *, nih, n_heads, n_kv, topk, idx_bits):
    i = pl.program_id(0)
    tq = q_ref.shape[0]
    group = n_heads // n_kv
    sub = 64

    for h in range(nih):
        wb_sc[h] = jnp.broadcast_to(iw_ref[:, h:h + 1], (tq, LANES))

    def scores(c):
        rhs = ikt_ref[c]
        halves = [jnp.zeros((tq, LANES), _f32) for _ in range(tq // LANES)]
        for h in range(nih):
            d = _dot(iq_ref[:, h * LANES:(h + 1) * LANES], rhs)
            w = wb_sc[h]
            for j in range(len(halves)):
                halves[j] = halves[j] + jnp.maximum(d[:, j * LANES:(j + 1) * LANES], 0.0) * w
        return _order_key(jnp.concatenate(halves, axis=1))

    def score_body(c, _):
        key_sc[c] = scores(c)
        return 0

    lax.fori_loop(0, i, score_body, 0)
    row = lax.broadcasted_iota(jnp.int32, (tq, tq), 0)
    col = lax.broadcasted_iota(jnp.int32, (tq, tq), 1)
    key_sc[i] = jnp.where(col <= row, scores(i), jnp.int32(INT_MIN))

    for rb in range(tq // sub):
        rs = pl.ds(rb * sub, sub)
        t = _select_threshold(lambda c: key_sc[c, rs, :],
                              lambda c, x: key_sc.__setitem__((c, rs, slice(None)), x),
                              i + 1, sub, tq, topk, idx_bits)
        thr_sc[rs, :] = jnp.broadcast_to(t, (sub, LANES))

    thr = jnp.concatenate([thr_sc[...]] * (tq // LANES), axis=1)
    m_sc[...] = jnp.full(m_sc.shape, NEG, _f32)
    l_sc[...] = jnp.zeros_like(l_sc)
    acc_sc[...] = jnp.zeros_like(acc_sc)

    def attend(c, _):
        bias = jnp.where(key_sc[c] >= thr, 0.0, NEG)
        vc = v_ref[c]
        for n in range(n_kv):
            kt = kt_ref[c, n]
            vn = vc[:, n * LANES:(n + 1) * LANES]
            for g in range(group):
                h = n * group + g
                lg = _dot(q_ref[:, h * LANES:(h + 1) * LANES], kt) + bias
                m_prev = m_sc[h]
                m_new = jnp.maximum(m_prev, jnp.max(lg, axis=1, keepdims=True))
                alpha = jnp.exp(m_prev - m_new)
                p = jnp.exp(lg - jnp.concatenate([m_new] * (tq // LANES), axis=1))
                l_sc[h] = alpha * l_sc[h] + jnp.sum(p, axis=1, keepdims=True)
                acc_sc[h] = alpha * acc_sc[h] + _dot(p.astype(_bf16), vn)
                m_sc[h] = m_new
        return 0

    lax.fori_loop(0, i + 1, attend, 0)
    for h in range(n_heads):
        o_ref[:, h * LANES:(h + 1) * LANES] = (acc_sc[h] / l_sc[h]).astype(_bf16)


def _prompt_attention(iq, iw, q, ikt, kt, v, *, tp, nih, n_heads, n_kv, topk):
    tq = Q_TILE
    nc = tp // tq
    resident = lambda shape: pl.BlockSpec(shape, lambda i: (0,) * len(shape),
                                          pipeline_mode=pl.Buffered(1))
    idx_bits = max(1, int(np.ceil(np.log2(tp + 1))))
    return pl.pallas_call(
        functools.partial(_prompt_attn_kernel, nih=nih, n_heads=n_heads, n_kv=n_kv, topk=topk,
                          idx_bits=idx_bits),
        grid=(nc,),
        in_specs=[pl.BlockSpec((tq, nih * LANES), lambda i: (i, 0)),
                  pl.BlockSpec((tq, nih), lambda i: (i, 0)),
                  pl.BlockSpec((tq, n_heads * LANES), lambda i: (i, 0)),
                  resident((nc, LANES, tq)),
                  resident((nc, n_kv, LANES, tq)),
                  resident((nc, tq, n_kv * LANES))],
        out_specs=pl.BlockSpec((tq, n_heads * LANES), lambda i: (i, 0)),
        out_shape=jax.ShapeDtypeStruct((tp, n_heads * LANES), _bf16),
        scratch_shapes=[pltpu.VMEM((nc, tq, tq), jnp.int32),
                        pltpu.VMEM((tq, LANES), jnp.int32),
                        pltpu.VMEM((nih, tq, LANES), _f32),
                        pltpu.VMEM((n_heads, tq, LANES), _f32),
                        pltpu.VMEM((n_heads, tq, LANES), _f32),
                        pltpu.VMEM((n_heads, tq, LANES), _f32)],
        compiler_params=_cparams(("arbitrary",)),
        name="prompt_attention",
    )(iq, iw, q, ikt, kt, v)


def _sample_attn_kernel(pt_ref, iq_ref, iw_ref, q_ref, ikn_ref, kn_ref, vn_ref, *refs,
                        pps, n_steps, n_pages, ds, nih, n_heads, n_kv, topk, idx_bits):
    ik_refs = refs[:pps]
    k_refs = refs[pps:2 * pps]
    v_refs = refs[2 * pps:3 * pps]
    o_ref = refs[3 * pps]
    key_sc, thr_sc, m_sc, l_sc, acc_sc = refs[3 * pps + 1:]
    s = pl.program_id(1)
    rows_i = nih * ds
    rows_q = n_heads * ds
    group = n_heads // n_kv

    def index_keys(ik_page):
        x = jnp.maximum(_dot_nt(iq_ref[0], ik_page), 0.0) * iw_ref[0]
        return _order_key(jnp.sum(x.reshape(nih, ds, LANES), axis=0))

    @pl.when(s < n_steps)
    def _():
        for j in range(pps):
            key_sc[s * pps + j] = index_keys(ik_refs[j][0].astype(_bf16))

    @pl.when(s == n_steps - 1)
    def _():
        qi = lax.broadcasted_iota(jnp.int32, (ds, LANES), 0)
        ki = lax.broadcasted_iota(jnp.int32, (ds, LANES), 1)
        key_sc[n_pages] = jnp.where(ki <= qi, index_keys(ikn_ref[0]), jnp.int32(INT_MIN))
        t = _select_threshold(lambda c: key_sc[c],
                              lambda c, x: key_sc.__setitem__(c, x),
                              n_pages + 1, ds, LANES, topk, idx_bits)
        thr_sc[...] = jnp.broadcast_to(t, (ds, LANES))
        m_sc[...] = jnp.full(m_sc.shape, NEG, _f32)
        l_sc[...] = jnp.zeros_like(l_sc)
        acc_sc[...] = jnp.zeros_like(acc_sc)

    def attend(page_ids, k_pages, v_pages):
        thr = thr_sc[...]
        qb = q_ref[0]
        lgs = []
        for c, kp in zip(page_ids, k_pages):
            bias = jnp.where(key_sc[c] >= thr, 0.0, NEG)
            lgs.append(_dot_nt(qb, kp) + jnp.concatenate([bias] * n_heads, axis=0))
        m_prev = m_sc[...]
        m_new = m_prev
        for lg in lgs:
            m_new = jnp.maximum(m_new, jnp.max(lg, axis=1, keepdims=True))
        alpha = jnp.exp(m_prev - m_new)
        l_new = alpha * l_sc[...]
        pv = jnp.zeros((rows_q, n_kv * LANES), _f32)
        for lg, vp in zip(lgs, v_pages):
            p = jnp.exp(lg - m_new)
            l_new = l_new + jnp.sum(p, axis=1, keepdims=True)
            pv = pv + _dot(p.astype(_bf16), vp)
        acc_sc[...] = jnp.concatenate([alpha] * n_kv, axis=1) * acc_sc[...] + pv
        l_sc[...] = l_new
        m_sc[...] = m_new

    @pl.when(s >= n_steps)
    def _():
        base = (s - n_steps) * pps
        attend([base + j for j in range(pps)],
               [k_refs[j][0].astype(_bf16) for j in range(pps)],
               [v_refs[j][0].astype(_bf16) for j in range(pps)])

    @pl.when(s == 2 * n_steps - 1)
    def _():
        attend([n_pages], [kn_ref[0]], [vn_ref[0]])
        inv = 1.0 / l_sc[...]
        for n in range(n_kv):
            rs = slice(n * group * ds, (n + 1) * group * ds)
            o_ref[0, rs, :] = (acc_sc[rs, n * LANES:(n + 1) * LANES] * inv[rs, :]).astype(_bf16)


def _sample_attention(page_table, iq_r, iw_r, q_bd, ik_new, k_new, v_new, cache_ik, cache_k, cache_v,
                      *, ds, nih, n_heads, n_kv, topk):
    db, n_pages = page_table.shape
    pps = min(PAGES_PER_STEP, n_pages)
    assert n_pages % pps == 0
    n_steps = n_pages // pps
    page = cache_ik.shape[1]
    assert page == LANES
    kvw = n_kv * LANES
    rows_i, rows_q = nih * ds, n_heads * ds
    idx_bits = max(1, int(np.ceil(np.log2((n_pages + 1) * LANES + 1))))
    pt_flat = page_table.reshape(-1)

    per_b = lambda shape: pl.BlockSpec((1,) + shape, lambda b, s, pt: (b,) + (0,) * len(shape))

    def ik_spec(j):
        return pl.BlockSpec((1, page, LANES),
                            lambda b, s, pt: (pt[b * n_pages + jnp.minimum(s, n_steps - 1) * pps + j], 0, 0))

    def kv_spec(j):
        return pl.BlockSpec((1, page, kvw),
                            lambda b, s, pt: (pt[b * n_pages + jnp.maximum(s - n_steps, 0) * pps + j], 0, 0))

    grid_spec = pltpu.PrefetchScalarGridSpec(
        num_scalar_prefetch=1,
        grid=(db, 2 * n_steps),
        in_specs=[per_b((rows_i, LANES)), per_b((rows_i, LANES)), per_b((rows_q, kvw)),
                  per_b((LANES, LANES)), per_b((LANES, kvw)), per_b((LANES, kvw))]
                 + [ik_spec(j) for j in range(pps)]
                 + [kv_spec(j) for j in range(pps)]
                 + [kv_spec(j) for j in range(pps)],
        out_specs=pl.BlockSpec((1, rows_q, LANES), lambda b, s, pt: (b, 0, 0)),
        scratch_shapes=[pltpu.VMEM((n_pages + 1, ds, LANES), jnp.int32),
                        pltpu.VMEM((ds, LANES), jnp.int32),
                        pltpu.VMEM((rows_q, LANES), _f32),
                        pltpu.VMEM((rows_q, LANES), _f32),
                        pltpu.VMEM((rows_q, kvw), _f32)],
    )
    return pl.pallas_call(
        functools.partial(_sample_attn_kernel, pps=pps, n_steps=n_steps, n_pages=n_pages, ds=ds,
                          nih=nih, n_heads=n_heads, n_kv=n_kv, topk=topk, idx_bits=idx_bits),
        grid_spec=grid_spec,
        out_shape=jax.ShapeDtypeStruct((db, rows_q, LANES), _bf16),
        compiler_params=_cparams(("parallel", "arbitrary")),
        name="sample_attention",
    )(pt_flat, iq_r, iw_r, q_bd, ik_new, k_new, v_new,
      *([cache_ik] * pps), *([cache_k] * pps), *([cache_v] * pps))


def _round_up(x, m):
    return (x + m - 1) // m * m


def _rope_tables(pos, dim):
    half = dim // 2
    inv = ROPE_THETA ** (-jnp.arange(half, dtype=_f32) / half)
    ang = pos.astype(_f32)[:, None] * inv[None, :]
    cos, sin = jnp.cos(ang), jnp.sin(ang)
    return jnp.concatenate([cos, cos], axis=1), jnp.concatenate([-sin, sin], axis=1)


def kernel(x_prompt, x_sample, cache_k, cache_v, cache_ik, state_conv, page_table, meta,
           norm1_g, ffn1_w1, ffn1_w3, ffn1_w2, norm_mix_g, w_in, w_dw, b_dw, conv_ln_g, conv_ln_b,
           w_conv_out, b_conv_out, w_o, w_out, norm2_g, ffn2_w1, ffn2_w3, ffn2_w2, final_g):
    bsz, seq, d = x_prompt.shape
    db, ds, _ = x_sample.shape
    depth, n_pool, page, n_kv, hd = cache_k.shape
    idx_dim = cache_ik.shape[-1]
    n_meta = meta.shape[0]
    width = w_dw.shape[1]
    ch = w_dw.shape[2]
    attn_w = w_o.shape[1]
    n_heads = attn_w // hd
    kv_w = n_kv * hd
    in_w = w_in.shape[2]
    nih = (in_w - 2 * ch - attn_w - 2 * kv_w - idx_dim - 2 * d) // (idx_dim + 1)
    n_pages = page_table.shape[1]
    past = n_pages * page
    assert bsz == 1 and depth == 1 and hd == LANES and idx_dim == LANES and page == LANES
    assert 2 * ch + attn_w + 2 * kv_w + nih * idx_dim + idx_dim + nih + 2 * d == in_w
    assert (n_heads * ds) % 8 == 0 and ds % 8 == 0 and ds <= LANES

    t = n_meta + seq
    tp = _round_up(t, Q_TILE)
    ms = db * ds
    mp = _round_up(tp + ms, ROW_TILE)
    topk_p = min(MAX_TOPK, seq // 4)
    topk_s = min(MAX_TOPK, (past + ds) // 4)
    assert topk_p <= Q_TILE and topk_s <= LANES * (n_pages + 1)

    h0 = jnp.concatenate([meta, x_prompt[0], jnp.zeros((tp - t, d), _f32),
                          x_sample.reshape(ms, d), jnp.zeros((mp - tp - ms, d), _f32)], axis=0)
    pos = jnp.concatenate([jnp.arange(t, dtype=jnp.int32), jnp.zeros((tp - t,), jnp.int32),
                           jnp.tile(past + jnp.arange(ds, dtype=jnp.int32), db),
                           jnp.zeros((mp - tp - ms,), jnp.int32)])
    cos, sin = _rope_tables(pos, hd)

    vec = lambda a: a.reshape(1, -1).astype(_f32)
    cast = lambda a: a.astype(_bf16)
    offs = np.cumsum([0, ch, ch, attn_w, kv_w, kv_w, nih * idx_dim, idx_dim, nih, d, d])
    wi = w_in[0]
    cols = lambda a, b: cast(wi[:, offs[a]:offs[b]])

    h1, u = _ffn(h0, vec(norm1_g[0]), cast(ffn1_w1[0]), cast(ffn1_w3[0]), cast(ffn1_w2[0]),
                 vec(norm_mix_g[0]), emit_h=True)

    glu = _proj_glu(u, cols(0, 1), cols(1, 2))
    (q,) = _proj(u, cols(2, 3), mode="rope", cos=cos, sin=sin, scale=hd ** -0.5,
                 out_dtypes=(_bf16,), name="proj_q")
    k32, k16 = _proj(u, cols(3, 4), mode="rope", cos=cos, sin=sin, out_dtypes=(_f32, _bf16), name="proj_k")
    v32, v16 = _proj(u, cols(4, 5), out_dtypes=(_f32, _bf16), name="proj_v")
    (iq,) = _proj(u, cols(5, 6), mode="rope", cos=cos, sin=sin, out_dtypes=(_bf16,), name="proj_iq")
    ik32, ik16 = _proj(u, cols(6, 7), mode="rope", cos=cos, sin=sin, out_dtypes=(_f32, _bf16), name="proj_ik")
    (iw,) = _proj(u, cols(7, 8), scale=(idx_dim ** -0.5) * (nih ** -0.5), name="proj_iw")
    (sg,) = _proj(u, cols(8, 10), mode="sigmoid", name="proj_gates")

    ca_p = _conv_prompt(glu, tp, w_dw[0], vec(b_dw[0]), vec(conv_ln_g[0]), vec(conv_ln_b[0]))
    glu_s = glu[tp:tp + ms].reshape(db, ds, ch)
    ext_rows = _round_up(width - 1 + ds, 8)
    ext_s = jnp.concatenate([state_conv[0], glu_s,
                             jnp.zeros((db, ext_rows - (width - 1 + ds), ch), _f32)], axis=1)
    ca_s = _conv_sample(ext_s, w_dw[0], vec(b_dw[0]), vec(conv_ln_g[0]), vec(conv_ln_b[0]), ds)
    ca = jnp.concatenate([ca_p, ca_s.reshape(ms, ch), jnp.zeros((mp - tp - ms, ch), _bf16)], axis=0)

    nc = tp // Q_TILE
    ikt = ik16[:tp].reshape(nc, Q_TILE, idx_dim).transpose(0, 2, 1)
    kt = k16[:tp].reshape(nc, Q_TILE, n_kv, hd).transpose(0, 2, 3, 1)
    vch = v16[:tp].reshape(nc, Q_TILE, kv_w)
    ao_p = _prompt_attention(iq, iw, q, ikt, kt, vch, tp=tp, nih=nih, n_heads=n_heads, n_kv=n_kv,
                             topk=topk_p)

    sl = slice(tp, tp + ms)
    iq_r = iq[sl].reshape(db, ds, nih, idx_dim).transpose(0, 2, 1, 3).reshape(db, nih * ds, idx_dim)
    iw_r = jnp.broadcast_to(iw[sl].reshape(db, ds, nih).transpose(0, 2, 1).reshape(db, nih * ds, 1),
                            (db, nih * ds, LANES))
    q_r = q[sl].reshape(db, ds, n_kv, n_heads // n_kv, hd).transpose(0, 2, 3, 1, 4)
    q_bd = (q_r[:, :, :, :, None, :] * jnp.eye(n_kv, dtype=_bf16)[None, :, None, None, :, None])
    q_bd = q_bd.reshape(db, n_heads * ds, kv_w)
    pad_keys = lambda a: jnp.pad(a.reshape(db, ds, -1), ((0, 0), (0, LANES - ds), (0, 0)))
    ao_s = _sample_attention(page_table, iq_r, iw_r, q_bd, pad_keys(ik16[sl]), pad_keys(k16[sl]),
                             pad_keys(v16[sl]),
                             cache_ik[0], cache_k[0].reshape(n_pool, page, kv_w),
                             cache_v[0].reshape(n_pool, page, kv_w),
                             ds=ds, nih=nih, n_heads=n_heads, n_kv=n_kv, topk=topk_s)
    ao_s = ao_s.reshape(db, n_heads, ds, hd).transpose(0, 2, 1, 3).reshape(ms, attn_w)
    ao = jnp.concatenate([ao_p, ao_s, jnp.zeros((mp - tp - ms, attn_w), _bf16)], axis=0)

    m = _merge(ca, ao, cast(w_conv_out[0]), cast(w_o[0]), vec(b_conv_out[0]), sg)
    h2 = _resid_proj(m, cast(w_out[0]), h1)
    y = _ffn(h2, vec(norm2_g[0]), cast(ffn2_w1[0]), cast(ffn2_w3[0]), cast(ffn2_w2[0]),
             vec(final_g), emit_h=False)

    y_prompt = y[n_meta:t][None]
    y_sample = y[sl].reshape(db, ds, d)
    hist = width - 1
    new_conv_p = glu[t - hist:t][None, None]
    new_conv_s = jnp.concatenate([state_conv[0], glu_s], axis=1)[:, -hist:][None]
    return (y_prompt, y_sample,
            k32[:t].reshape(1, 1, t, n_kv, hd), v32[:t].reshape(1, 1, t, n_kv, hd),
            ik32[:t].reshape(1, 1, t, idx_dim), new_conv_p,
            k32[sl].reshape(1, db, ds, n_kv, hd), v32[sl].reshape(1, db, ds, n_kv, hd),
            ik32[sl].reshape(1, db, ds, idx_dim), new_conv_s)
```

```python
import functools

import numpy as np
import jax
import jax.numpy as jnp
from jax import lax
from jax.experimental import pallas as pl
from jax.experimental.pallas import tpu as pltpu

MAX_TOPK = 256
ROPE_THETA = 10000.0
EPS = 1e-6

LANES = 128
ROW_TILE = 512
Q_TILE = 256
FF_TILE = 512
PAGES_PER_STEP_INDEX = 32
PAGES_PER_STEP_ATTEND = 8
VMEM_LIMIT = 56 * 1024 * 1024
INT_MIN = -(2 ** 31)
NEG = -1e30
LOG2E = 1.4426950408889634

_bf16 = jnp.bfloat16
_f32 = jnp.float32


def _cparams(sem):
    return pltpu.CompilerParams(dimension_semantics=sem, vmem_limit_bytes=VMEM_LIMIT)


def _dot(a, b):
    return jnp.dot(a, b, preferred_element_type=_f32)


def _dot_nt(a, b):
    return lax.dot_general(a, b, (((1,), (1,)), ((), ())), preferred_element_type=_f32)


def _rms(x, g):
    return x * lax.rsqrt(jnp.mean(x * x, axis=-1, keepdims=True) + EPS) * g


def _sigmoid(x):
    return 1.0 / (1.0 + jnp.exp(-x))


def _ffn_kernel(h_ref, g_ref, w1_ref, w3_ref, w2_ref, g2_ref, *refs, n_f, emit_h):
    if emit_h:
        h_out_ref, u_out_ref, u_sc, acc_sc = refs
    else:
        y_out_ref, u_sc, acc_sc = refs
    f = pl.program_id(1)

    @pl.when(f == 0)
    def _():
        u_sc[...] = _rms(h_ref[...], g_ref[...]).astype(_bf16)
        acc_sc[...] = jnp.zeros_like(acc_sc)

    u = u_sc[...]
    a = _dot(u, w1_ref[...])
    b = _dot(u, w3_ref[...])
    gate = (a * _sigmoid(a)) * b
    acc_sc[...] += _dot(gate.astype(_bf16), w2_ref[...])

    @pl.when(f == n_f - 1)
    def _():
        hn = h_ref[...] + 0.5 * acc_sc[...]
        un = _rms(hn, g2_ref[...])
        if emit_h:
            h_out_ref[...] = hn
            u_out_ref[...] = un.astype(_bf16)
        else:
            y_out_ref[...] = un


def _ffn(h, g, w1, w3, w2, g2, *, emit_h):
    mp, d = h.shape
    dff = w1.shape[1]
    tf = min(FF_TILE, dff)
    assert mp % ROW_TILE == 0 and dff % tf == 0
    n_f = dff // tf
    row = pl.BlockSpec((ROW_TILE, d), lambda i, f: (i, 0))
    vec = pl.BlockSpec((1, d), lambda i, f: (0, 0))
    if emit_h:
        out_shape = (jax.ShapeDtypeStruct((mp, d), _f32), jax.ShapeDtypeStruct((mp, d), _bf16))
        out_specs = (row, row)
    else:
        out_shape = jax.ShapeDtypeStruct((mp, d), _f32)
        out_specs = row
    return pl.pallas_call(
        functools.partial(_ffn_kernel, n_f=n_f, emit_h=emit_h),
        grid=(mp // ROW_TILE, n_f),
        in_specs=[row, vec,
                  pl.BlockSpec((d, tf), lambda i, f: (0, f)),
                  pl.BlockSpec((d, tf), lambda i, f: (0, f)),
                  pl.BlockSpec((tf, d), lambda i, f: (f, 0)),
                  vec],
        out_specs=out_specs,
        out_shape=out_shape,
        scratch_shapes=[pltpu.VMEM((ROW_TILE, d), _bf16), pltpu.VMEM((ROW_TILE, d), _f32)],
        compiler_params=_cparams(("parallel", "arbitrary")),
        name="ffn_emit_h" if emit_h else "ffn_final",
    )(h, g, w1, w3, w2, g2)


def _glu_kernel(u_ref, wa_ref, wb_ref, o_ref):
    u = u_ref[...]
    o_ref[...] = _dot(u, wa_ref[...]) * _sigmoid(_dot(u, wb_ref[...]))


def _proj_glu(u, wa, wb):
    mp, d = u.shape
    n = wa.shape[1]
    tn = min(512, n)
    assert n % tn == 0
    return pl.pallas_call(
        _glu_kernel,
        grid=(mp // ROW_TILE, n // tn),
        in_specs=[pl.BlockSpec((ROW_TILE, d), lambda i, j: (i, 0)),
                  pl.BlockSpec((d, tn), lambda i, j: (0, j)),
                  pl.BlockSpec((d, tn), lambda i, j: (0, j))],
        out_specs=pl.BlockSpec((ROW_TILE, tn), lambda i, j: (i, j)),
        out_shape=jax.ShapeDtypeStruct((mp, n), _f32),
        compiler_params=_cparams(("parallel", "parallel")),
        name="proj_glu",
    )(u, wa, wb)


def _proj_kernel(u_ref, w_ref, *refs, mode, scale, out_dtypes):
    if mode == "rope":
        cos_ref, sin_ref = refs[:2]
        outs = refs[2:]
    else:
        outs = refs
    z = _dot(u_ref[...], w_ref[...])
    if mode == "sigmoid":
        z = _sigmoid(z)
    elif mode == "rope":
        cos = cos_ref[...]
        sin = sin_ref[...]
        parts = []
        for j in range(z.shape[1] // LANES):
            x = z[:, j * LANES:(j + 1) * LANES]
            parts.append(x * cos + pltpu.roll(x, LANES // 2, 1) * sin)
        z = parts[0] if len(parts) == 1 else jnp.concatenate(parts, axis=1)
    if scale != 1.0:
        z = z * scale
    for o_ref, dt in zip(outs, out_dtypes):
        o_ref[...] = z.astype(dt)


def _proj(u, w, *, mode="plain", cos=None, sin=None, scale=1.0, out_dtypes=(_f32,), name="proj"):
    mp, d = u.shape
    n = w.shape[1]
    tn = n if n <= 512 else 512
    assert n % tn == 0
    in_specs = [pl.BlockSpec((ROW_TILE, d), lambda i, j: (i, 0)),
                pl.BlockSpec((d, tn), lambda i, j: (0, j))]
    args = [u, w]
    if mode == "rope":
        in_specs += [pl.BlockSpec((ROW_TILE, LANES), lambda i, j: (i, 0))] * 2
        args += [cos, sin]
    ospec = pl.BlockSpec((ROW_TILE, tn), lambda i, j: (i, j))
    outs = pl.pallas_call(
        functools.partial(_proj_kernel, mode=mode, scale=scale, out_dtypes=out_dtypes),
        grid=(mp // ROW_TILE, n // tn),
        in_specs=in_specs,
        out_specs=tuple(ospec for _ in out_dtypes),
        out_shape=tuple(jax.ShapeDtypeStruct((mp, n), dt) for dt in out_dtypes),
        compiler_params=_cparams(("parallel", "parallel")),
        name=name,
    )(*args)
    return outs


def _ln_swish(c, g, b):
    mu = jnp.mean(c, axis=-1, keepdims=True)
    var = jnp.mean(jnp.square(c - mu), axis=-1, keepdims=True)
    y = (c - mu) * lax.rsqrt(var + EPS) * g + b
    return y * _sigmoid(y)


def _conv_prompt_kernel(cur_ref, halo_ref, w_ref, bdw_ref, g_ref, b_ref, o_ref, ext_sc, sh_sc, c_sc,
                        *, width, tc, halo, cblk):
    i = pl.program_id(0)
    ch = cur_ref.shape[1]
    ext_sc[pl.ds(0, halo), :] = jnp.where(i > 0, halo_ref[...], 0.0)
    ext_sc[pl.ds(halo, tc), :] = cur_ref[...]
    ext_sc[pl.ds(halo + tc, 8), :] = jnp.zeros((8, ch), _f32)
    lead = halo - (width - 1)
    for cb in range(ch // cblk):
        cs = pl.ds(cb * cblk, cblk)
        for r in range(8):
            sh_sc[r] = ext_sc[pl.ds(r, halo + tc), cs]

        def rows(t, _):
            t0 = pl.multiple_of(t * 8, 8)
            acc = jnp.broadcast_to(bdw_ref[:, cs], (8, cblk))
            for j in range(width):
                off = lead + j
                x = sh_sc[off % 8, pl.ds(t0 + (off // 8) * 8, 8), :]
                acc = acc + x * w_ref[pl.ds(j, 1), cs]
            c_sc[pl.ds(t0, 8), cs] = acc
            return 0

        lax.fori_loop(0, tc // 8, rows, 0)
    o_ref[...] = _ln_swish(c_sc[...], g_ref[...], b_ref[...]).astype(_bf16)


def _conv_prompt(glu, tp, w_dw, b_dw, ln_g, ln_b):
    ch = glu.shape[1]
    width = w_dw.shape[0]
    tc = Q_TILE
    halo = 32
    assert width - 1 <= halo and tp % tc == 0 and tc % halo == 0
    cblk = min(512, ch)
    vec = pl.BlockSpec((1, ch), lambda i: (0, 0))
    return pl.pallas_call(
        functools.partial(_conv_prompt_kernel, width=width, tc=tc, halo=halo, cblk=cblk),
        grid=(tp // tc,),
        in_specs=[pl.BlockSpec((tc, ch), lambda i: (i, 0)),
                  pl.BlockSpec((halo, ch), lambda i: (jnp.maximum(i * (tc // halo) - 1, 0), 0)),
                  pl.BlockSpec((width, ch), lambda i: (0, 0)),
                  vec, vec, vec],
        out_specs=pl.BlockSpec((tc, ch), lambda i: (i, 0)),
        out_shape=jax.ShapeDtypeStruct((tp, ch), _bf16),
        scratch_shapes=[pltpu.VMEM((halo + tc + 8, ch), _f32),
                        pltpu.VMEM((8, halo + tc, cblk), _f32),
                        pltpu.VMEM((tc, ch), _f32)],
        compiler_params=_cparams(("parallel",)),
        name="conv_prompt",
    )(glu, glu, w_dw, b_dw, ln_g, ln_b)


def _conv_sample_kernel(ext_ref, w_ref, bdw_ref, g_ref, b_ref, o_ref, *, width, ds):
    ch = ext_ref.shape[2]
    acc = jnp.broadcast_to(bdw_ref[...], (ds, ch))
    for j in range(width):
        acc = acc + ext_ref[0, pl.ds(j, ds), :] * w_ref[pl.ds(j, 1), :]
    o_ref[0] = _ln_swish(acc, g_ref[...], b_ref[...]).astype(_bf16)


def _conv_sample(ext, w_dw, b_dw, ln_g, ln_b, ds):
    db, rows, ch = ext.shape
    width = w_dw.shape[0]
    vec = pl.BlockSpec((1, ch), lambda b: (0, 0))
    return pl.pallas_call(
        functools.partial(_conv_sample_kernel, width=width, ds=ds),
        grid=(db,),
        in_specs=[pl.BlockSpec((1, rows, ch), lambda b: (b, 0, 0)),
                  pl.BlockSpec((width, ch), lambda b: (0, 0)),
                  vec, vec, vec],
        out_specs=pl.BlockSpec((1, ds, ch), lambda b: (b, 0, 0)),
        out_shape=jax.ShapeDtypeStruct((db, ds, ch), _bf16),
        compiler_params=_cparams(("parallel",)),
        name="conv_sample",
    )(ext, w_dw, b_dw, ln_g, ln_b)


def _merge_kernel(ca_ref, ao_ref, wco_ref, wo_ref, bco_ref, sgc_ref, sga_ref, m_ref):
    yc = _dot(ca_ref[...], wco_ref[...]) + bco_ref[...]
    ya = _dot(ao_ref[...], wo_ref[...])
    m_ref[...] = (sgc_ref[...] * yc + sga_ref[...] * ya).astype(_bf16)


def _merge(ca, ao, wco, wo, bco, sg):
    mp, ch = ca.shape
    aw = ao.shape[1]
    d = wco.shape[1]
    tn = min(512, d)
    nj = d // tn
    return pl.pallas_call(
        _merge_kernel,
        grid=(mp // ROW_TILE, nj),
        in_specs=[pl.BlockSpec((ROW_TILE, ch), lambda i, j: (i, 0)),
                  pl.BlockSpec((ROW_TILE, aw), lambda i, j: (i, 0)),
                  pl.BlockSpec((ch, tn), lambda i, j: (0, j)),
                  pl.BlockSpec((aw, tn), lambda i, j: (0, j)),
                  pl.BlockSpec((1, tn), lambda i, j: (0, j)),
                  pl.BlockSpec((ROW_TILE, tn), lambda i, j: (i, j)),
                  pl.BlockSpec((ROW_TILE, tn), lambda i, j: (i, j + nj))],
        out_specs=pl.BlockSpec((ROW_TILE, tn), lambda i, j: (i, j)),
        out_shape=jax.ShapeDtypeStruct((mp, d), _bf16),
        compiler_params=_cparams(("parallel", "parallel")),
        name="merge",
    )(ca, ao, wco, wo, bco, sg, sg)


def _resid_kernel(m_ref, w_ref, h_ref, o_ref):
    o_ref[...] = h_ref[...] + _dot(m_ref[...], w_ref[...])


def _resid_proj(m, w, h):
    mp, d = m.shape
    n = w.shape[1]
    tn = min(512, n)
    return pl.pallas_call(
        _resid_kernel,
        grid=(mp // ROW_TILE, n // tn),
        in_specs=[pl.BlockSpec((ROW_TILE, d), lambda i, j: (i, 0)),
                  pl.BlockSpec((d, tn), lambda i, j: (0, j)),
                  pl.BlockSpec((ROW_TILE, tn), lambda i, j: (i, j))],
        out_specs=pl.BlockSpec((ROW_TILE, tn), lambda i, j: (i, j)),
        out_shape=jax.ShapeDtypeStruct((mp, n), _f32),
        compiler_params=_cparams(("parallel", "parallel")),
        name="resid_proj",
    )(m, w, h)


def _order_key(s):
    b = lax.bitcast_convert_type(s, jnp.int32)
    return b ^ (lax.shift_right_arithmetic(b, 31) & jnp.int32(0x7FFFFFFF))


def _bit_search(count_ge, rows, n_bits, k, signed):
    flip = jnp.int32(INT_MIN) if signed else jnp.int32(0)

    def body(b, t_u):
        bit = lax.shift_left(jnp.int32(1), (n_bits - 1 - b).astype(jnp.int32))
        cand = t_u | bit
        cnt = count_ge(cand ^ flip)
        return jnp.where(cnt >= k, cand, t_u)

    t_u = lax.fori_loop(0, n_bits, body, jnp.zeros((rows, 1), jnp.int32))
    return t_u ^ flip


def _select_threshold(load, store, n_iters, unroll, rows, width, k, idx_bits):
    halves = width // LANES

    def scan(fn, init):
        def body(it, carry):
            for u in range(unroll):
                carry = fn(it * unroll + u, carry)
            return carry
        return lax.fori_loop(0, n_iters, body, init)

    def count(pred):
        def one(c, acc):
            m = pred(load(c), c)
            for j in range(halves):
                acc = acc + jnp.where(m[:, j * LANES:(j + 1) * LANES], 1.0, 0.0)
            return acc
        return jnp.sum(scan(one, jnp.zeros((rows, LANES), _f32)), axis=1, keepdims=True)

    kf = jnp.float32(k)
    t = _bit_search(lambda th: count(lambda x, c: x >= th), rows, 32, kf, True)
    t = jnp.maximum(t, jnp.int32(INT_MIN + 1))
    n_ge = count(lambda x, c: x >= t)

    @pl.when(jnp.max(n_ge) > kf)
    def _():
        n_gt = count(lambda x, c: x > t)
        need = kf - n_gt
        lane = lax.broadcasted_iota(jnp.int32, (rows, width), 1)

        def ties_below(j):
            return count(lambda x, c: (x == t) & ((lane + c * width) < j))

        j_cut = _bit_search(lambda j: need - ties_below(j) + (kf - 1.0), rows, idx_bits, kf, False)
        surplus = n_ge > kf

        def rewrite(c, carry):
            x = load(c)
            drop = (x == t) & ((lane + c * width) > j_cut) & surplus
            store(c, jnp.where(drop, jnp.int32(INT_MIN), x))
            return carry
        scan(rewrite, 0)

    return t


def _prompt_attn_kernel(iq_ref, iw_ref, q_ref, ikt_ref, kt_ref, v_ref, o_ref,
                        key_sc, thr_sc, wb_sc, m_sc, acc_sc, *, nih, n_heads, n_kv, topk, idx_bits):
    i = pl.program_id(0)
    tq = q_ref.shape[0]
    group = n_heads // n_kv
    sub = 64
    unroll = 2

    for h in range(nih):
        wb_sc[h] = jnp.broadcast_to(iw_ref[:, h:h + 1], (tq, LANES))

    def scores(c):
        rhs = ikt_ref[c]
        halves = [jnp.zeros((tq, LANES), _f32) for _ in range(tq // LANES)]
        for h in range(nih):
            d = _dot(iq_ref[:, h * LANES:(h + 1) * LANES], rhs)
            w = wb_sc[h]
            for j in range(len(halves)):
                halves[j] = halves[j] + jnp.maximum(d[:, j * LANES:(j + 1) * LANES], 0.0) * w
        return _order_key(jnp.concatenate(halves, axis=1))

    def score_body(c, _):
        key_sc[c] = scores(c)
        return 0

    lax.fori_loop(0, i, score_body, 0)
    row = lax.broadcasted_iota(jnp.int32, (tq, tq), 0)
    col = lax.broadcasted_iota(jnp.int32, (tq, tq), 1)
    key_sc[i] = jnp.where(col <= row, scores(i), jnp.int32(INT_MIN))
    key_sc[i + 1] = jnp.full((tq, tq), INT_MIN, jnp.int32)
    n_iters = lax.shift_right_logical(i + unroll, 1)

    for rb in range(tq // sub):
        rs = pl.ds(rb * sub, sub)
        t = _select_threshold(lambda c: key_sc[c, rs, :],
                              lambda c, x: key_sc.__setitem__((c, rs, slice(None)), x),
                              n_iters, unroll, sub, tq, topk, idx_bits)
        thr_sc[rs, :] = jnp.broadcast_to(t, (sub, LANES))

    thr = jnp.concatenate([thr_sc[...]] * (tq // LANES), axis=1)
    m_sc[...] = jnp.full(m_sc.shape, NEG, _f32)
    acc_sc[...] = jnp.zeros_like(acc_sc)
    ones = jnp.ones((tq, LANES), _bf16)

    def attend(c, _):
        bias = jnp.where(key_sc[c] >= thr, 0.0, NEG)
        vc = v_ref[c]
        for n in range(n_kv):
            kt = kt_ref[c, n]
            vn1 = jnp.concatenate([vc[:, n * LANES:(n + 1) * LANES], ones], axis=1)
            for g in range(group):
                h = n * group + g
                lg = _dot(q_ref[:, h * LANES:(h + 1) * LANES], kt) + bias
                m_prev = m_sc[h]
                m_new = jnp.maximum(m_prev, jnp.max(lg, axis=1, keepdims=True))
                alpha = jnp.exp2(m_prev - m_new)
                p = jnp.exp2(lg - jnp.concatenate([m_new] * (tq // LANES), axis=1))
                acc_sc[h] = jnp.concatenate([alpha, alpha], axis=1) * acc_sc[h] + _dot(p.astype(_bf16), vn1)
                m_sc[h] = m_new
        return 0

    lax.fori_loop(0, i + 1, attend, 0)
    for h in range(n_heads):
        a = acc_sc[h]
        o_ref[:, h * LANES:(h + 1) * LANES] = (a[:, :LANES] / a[:, LANES:]).astype(_bf16)


def _prompt_attention(iq, iw, q, ikt, kt, v, *, tp, nih, n_heads, n_kv, topk):
    tq = Q_TILE
    nc = tp // tq
    resident = lambda shape: pl.BlockSpec(shape, lambda i: (0,) * len(shape),
                                          pipeline_mode=pl.Buffered(1))
    idx_bits = max(1, int(np.ceil(np.log2(tp + 1))))
    return pl.pallas_call(
        functools.partial(_prompt_attn_kernel, nih=nih, n_heads=n_heads, n_kv=n_kv, topk=topk,
                          idx_bits=idx_bits),
        grid=(nc,),
        in_specs=[pl.BlockSpec((tq, nih * LANES), lambda i: (i, 0)),
                  pl.BlockSpec((tq, nih), lambda i: (i, 0)),
                  pl.BlockSpec((tq, n_heads * LANES), lambda i: (i, 0)),
                  resident((nc, LANES, tq)),
                  resident((nc, n_kv, LANES, tq)),
                  resident((nc, tq, n_kv * LANES))],
        out_specs=pl.BlockSpec((tq, n_heads * LANES), lambda i: (i, 0)),
        out_shape=jax.ShapeDtypeStruct((tp, n_heads * LANES), _bf16),
        scratch_shapes=[pltpu.VMEM((nc + 1, tq, tq), jnp.int32),
                        pltpu.VMEM((tq, LANES), jnp.int32),
                        pltpu.VMEM((nih, tq, LANES), _f32),
                        pltpu.VMEM((n_heads, tq, LANES), _f32),
                        pltpu.VMEM((n_heads, tq, 2 * LANES), _f32)],
        compiler_params=_cparams(("arbitrary",)),
        name="prompt_attention",
    )(iq, iw, q, ikt, kt, v)


def _sample_attn_kernel(pt_ref, iq_ref, iw_ref, q_ref, ikn_ref, kn_ref, vn_ref, *refs,
                        pps_a, pps_b, n_a, n_b, n_pages, ds, nih, n_heads, n_kv, topk, idx_bits):
    ik_refs = refs[:pps_a]
    k_refs = refs[pps_a:pps_a + pps_b]
    v_refs = refs[pps_a + pps_b:pps_a + 2 * pps_b]
    o_ref = refs[pps_a + 2 * pps_b]
    key_sc, thr_sc, m_sc, l_sc, acc_sc = refs[pps_a + 2 * pps_b + 1:]
    s = pl.program_id(1)
    group = n_heads // n_kv
    rows_g = group * ds
    page = ik_refs[0].shape[1]

    def index_keys(ik_page):
        x = jnp.maximum(_dot_nt(iq_ref[0], ik_page), 0.0) * iw_ref[0]
        return _order_key(jnp.sum(x.reshape(nih, ds, LANES), axis=0))

    @pl.when(s < n_a)
    def _():
        for j in range(pps_a):
            key_sc[s * pps_a + j] = index_keys(ik_refs[j][0].astype(_bf16))

    @pl.when(s == n_a - 1)
    def _():
        qi = lax.broadcasted_iota(jnp.int32, (ds, LANES), 0)
        ki = lax.broadcasted_iota(jnp.int32, (ds, LANES), 1)
        key_sc[n_pages] = jnp.where(ki <= qi, index_keys(ikn_ref[0]), jnp.int32(INT_MIN))
        t = _select_threshold(lambda c: key_sc[c],
                              lambda c, x: key_sc.__setitem__(c, x),
                              n_pages + 1, 1, ds, LANES, topk, idx_bits)
        thr_sc[...] = jnp.broadcast_to(t, (ds, LANES))
        m_sc[...] = jnp.full(m_sc.shape, NEG, _f32)
        l_sc[...] = jnp.zeros_like(l_sc)
        acc_sc[...] = jnp.zeros_like(acc_sc)

    def attend(page_ids, k_of, v_of):
        thr = thr_sc[...]
        biases = [jnp.concatenate([jnp.where(key_sc[c] >= thr, 0.0, NEG)] * group, axis=0)
                  for c in page_ids]
        for n in range(n_kv):
            rs = slice(n * rows_g, (n + 1) * rows_g)
            qn = q_ref[0, rs, :]
            lgs = [_dot_nt(qn, k_of(j, n)) + biases[j] for j in range(len(page_ids))]
            m_prev = m_sc[rs, :]
            m_new = m_prev
            for lg in lgs:
                m_new = jnp.maximum(m_new, jnp.max(lg, axis=1, keepdims=True))
            alpha = jnp.exp2(m_prev - m_new)
            l_new = alpha * l_sc[rs, :]
            pv = jnp.zeros((rows_g, LANES), _f32)
            for j, lg in enumerate(lgs):
                p = jnp.exp2(lg - m_new)
                l_new = l_new + jnp.sum(p, axis=1, keepdims=True)
                pv = pv + _dot(p.astype(_bf16), v_of(j, n))
            acc_sc[rs, :] = alpha * acc_sc[rs, :] + pv
            l_sc[rs, :] = l_new
            m_sc[rs, :] = m_new

    def head_rows(ref, n):
        return ref[0, pl.ds(n, page, stride=n_kv), :].astype(_bf16)

    @pl.when(s >= n_a)
    def _():
        base = (s - n_a) * pps_b
        attend([base + j for j in range(pps_b)],
               lambda j, n: head_rows(k_refs[j], n), lambda j, n: head_rows(v_refs[j], n))

    @pl.when(s == n_a + n_b - 1)
    def _():
        attend([n_pages],
               lambda j, n: kn_ref[0, :, n * LANES:(n + 1) * LANES],
               lambda j, n: vn_ref[0, :, n * LANES:(n + 1) * LANES])
        o_ref[0] = (acc_sc[...] / l_sc[...]).astype(_bf16)


def _sample_attention(page_table, iq_r, iw_r, q_r, ik_new, k_new, v_new, cache_ik, cache_k, cache_v,
                      *, ds, nih, n_heads, n_kv, topk):
    db, n_pages = page_table.shape
    pps_a = min(PAGES_PER_STEP_INDEX, n_pages)
    pps_b = min(PAGES_PER_STEP_ATTEND, n_pages)
    assert n_pages % pps_a == 0 and n_pages % pps_b == 0
    n_a, n_b = n_pages // pps_a, n_pages // pps_b
    page = cache_ik.shape[1]
    assert page == LANES
    kvw = n_kv * LANES
    rows_i, rows_q = nih * ds, n_heads * ds
    idx_bits = max(1, int(np.ceil(np.log2((n_pages + 1) * LANES + 1))))
    pt_flat = page_table.reshape(-1)

    per_b = lambda shape: pl.BlockSpec((1,) + shape, lambda b, s, pt: (b,) + (0,) * len(shape))

    def ik_spec(j):
        return pl.BlockSpec((1, page, LANES),
                            lambda b, s, pt: (pt[b * n_pages + jnp.minimum(s, n_a - 1) * pps_a + j], 0, 0))

    def kv_spec(j):
        return pl.BlockSpec((1, page * n_kv, LANES),
                            lambda b, s, pt: (pt[b * n_pages + jnp.maximum(s - n_a, 0) * pps_b + j], 0, 0))

    grid_spec = pltpu.PrefetchScalarGridSpec(
        num_scalar_prefetch=1,
        grid=(db, n_a + n_b),
        in_specs=[per_b((rows_i, LANES)), per_b((rows_i, LANES)), per_b((rows_q, LANES)),
                  per_b((LANES, LANES)), per_b((LANES, kvw)), per_b((LANES, kvw))]
                 + [ik_spec(j) for j in range(pps_a)]
                 + [kv_spec(j) for j in range(pps_b)]
                 + [kv_spec(j) for j in range(pps_b)],
        out_specs=pl.BlockSpec((1, rows_q, LANES), lambda b, s, pt: (b, 0, 0)),
        scratch_shapes=[pltpu.VMEM((n_pages + 1, ds, LANES), jnp.int32),
                        pltpu.VMEM((ds, LANES), jnp.int32),
                        pltpu.VMEM((rows_q, LANES), _f32),
                        pltpu.VMEM((rows_q, LANES), _f32),
                        pltpu.VMEM((rows_q, LANES), _f32)],
    )
    return pl.pallas_call(
        functools.partial(_sample_attn_kernel, pps_a=pps_a, pps_b=pps_b, n_a=n_a, n_b=n_b,
                          n_pages=n_pages, ds=ds, nih=nih, n_heads=n_heads, n_kv=n_kv, topk=topk,
                          idx_bits=idx_bits),
        grid_spec=grid_spec,
        out_shape=jax.ShapeDtypeStruct((db, rows_q, LANES), _bf16),
        compiler_params=_cparams(("parallel", "arbitrary")),
        name="sample_attention",
    )(pt_flat, iq_r, iw_r, q_r, ik_new, k_new, v_new,
      *([cache_ik] * pps_a), *([cache_k] * pps_b), *([cache_v] * pps_b))


def _round_up(x, m):
    return (x + m - 1) // m * m


def _rope_tables(pos, dim):
    half = dim // 2
    inv = ROPE_THETA ** (-jnp.arange(half, dtype=_f32) / half)
    ang = pos.astype(_f32)[:, None] * inv[None, :]
    cos, sin = jnp.cos(ang), jnp.sin(ang)
    return jnp.concatenate([cos, cos], axis=1), jnp.concatenate([-sin, sin], axis=1)


def kernel(x_prompt, x_sample, cache_k, cache_v, cache_ik, state_conv, page_table, meta,
           norm1_g, ffn1_w1, ffn1_w3, ffn1_w2, norm_mix_g, w_in, w_dw, b_dw, conv_ln_g, conv_ln_b,
           w_conv_out, b_conv_out, w_o, w_out, norm2_g, ffn2_w1, ffn2_w3, ffn2_w2, final_g):
    bsz, seq, d = x_prompt.shape
    db, ds, _ = x_sample.shape
    depth, n_pool, page, n_kv, hd = cache_k.shape
    idx_dim = cache_ik.shape[-1]
    n_meta = meta.shape[0]
    width = w_dw.shape[1]
    ch = w_dw.shape[2]
    attn_w = w_o.shape[1]
    n_heads = attn_w // hd
    kv_w = n_kv * hd
    in_w = w_in.shape[2]
    nih = (in_w - 2 * ch - attn_w - 2 * kv_w - idx_dim - 2 * d) // (idx_dim + 1)
    n_pages = page_table.shape[1]
    past = n_pages * page
    assert bsz == 1 and depth == 1 and hd == LANES and idx_dim == LANES and page == LANES
    assert 2 * ch + attn_w + 2 * kv_w + nih * idx_dim + idx_dim + nih + 2 * d == in_w
    assert (n_heads * ds) % 8 == 0 and ds % 8 == 0 and ds <= LANES

    t = n_meta + seq
    tp = _round_up(t, Q_TILE)
    ms = db * ds
    mp = _round_up(tp + ms, ROW_TILE)
    topk_p = min(MAX_TOPK, seq // 4)
    topk_s = min(MAX_TOPK, (past + ds) // 4)
    assert topk_p <= Q_TILE and topk_s <= LANES * (n_pages + 1)

    h0 = jnp.concatenate([meta, x_prompt[0], jnp.zeros((tp - t, d), _f32),
                          x_sample.reshape(ms, d), jnp.zeros((mp - tp - ms, d), _f32)], axis=0)
    pos = jnp.concatenate([jnp.arange(t, dtype=jnp.int32), jnp.zeros((tp - t,), jnp.int32),
                           jnp.tile(past + jnp.arange(ds, dtype=jnp.int32), db),
                           jnp.zeros((mp - tp - ms,), jnp.int32)])
    cos, sin = _rope_tables(pos, hd)

    vec = lambda a: a.reshape(1, -1).astype(_f32)
    cast = lambda a: a.astype(_bf16)
    offs = np.cumsum([0, ch, ch, attn_w, kv_w, kv_w, nih * idx_dim, idx_dim, nih, d, d])
    wi = w_in[0]
    cols = lambda a, b: cast(wi[:, offs[a]:offs[b]])

    h1, u = _ffn(h0, vec(norm1_g[0]), cast(ffn1_w1[0]), cast(ffn1_w3[0]), cast(ffn1_w2[0]),
                 vec(norm_mix_g[0]), emit_h=True)

    glu = _proj_glu(u, cols(0, 1), cols(1, 2))
    (q,) = _proj(u, cols(2, 3), mode="rope", cos=cos, sin=sin, scale=hd ** -0.5 * LOG2E,
                 out_dtypes=(_bf16,), name="proj_q")
    k32, k16 = _proj(u, cols(3, 4), mode="rope", cos=cos, sin=sin, out_dtypes=(_f32, _bf16), name="proj_k")
    v32, v16 = _proj(u, cols(4, 5), out_dtypes=(_f32, _bf16), name="proj_v")
    (iq,) = _proj(u, cols(5, 6), mode="rope", cos=cos, sin=sin, out_dtypes=(_bf16,), name="proj_iq")
    ik32, ik16 = _proj(u, cols(6, 7), mode="rope", cos=cos, sin=sin, out_dtypes=(_f32, _bf16), name="proj_ik")
    (iw,) = _proj(u, cols(7, 8), scale=(idx_dim ** -0.5) * (nih ** -0.5), name="proj_iw")
    (sg,) = _proj(u, cols(8, 10), mode="sigmoid", name="proj_gates")

    ca_p = _conv_prompt(glu, tp, w_dw[0], vec(b_dw[0]), vec(conv_ln_g[0]), vec(conv_ln_b[0]))
    glu_s = glu[tp:tp + ms].reshape(db, ds, ch)
    ext_rows = _round_up(width - 1 + ds, 8)
    ext_s = jnp.concatenate([state_conv[0], glu_s,
                             jnp.zeros((db, ext_rows - (width - 1 + ds), ch), _f32)], axis=1)
    ca_s = _conv_sample(ext_s, w_dw[0], vec(b_dw[0]), vec(conv_ln_g[0]), vec(conv_ln_b[0]), ds)
    ca = jnp.concatenate([ca_p, ca_s.reshape(ms, ch), jnp.zeros((mp - tp - ms, ch), _bf16)], axis=0)

    nc = tp // Q_TILE
    ikt = ik16[:tp].reshape(nc, Q_TILE, idx_dim).transpose(0, 2, 1)
    kt = k16[:tp].reshape(nc, Q_TILE, n_kv, hd).transpose(0, 2, 3, 1)
    vch = v16[:tp].reshape(nc, Q_TILE, kv_w)
    ao_p = _prompt_attention(iq, iw, q, ikt, kt, vch, tp=tp, nih=nih, n_heads=n_heads, n_kv=n_kv,
                             topk=topk_p)

    sl = slice(tp, tp + ms)
    iq_r = iq[sl].reshape(db, ds, nih, idx_dim).transpose(0, 2, 1, 3).reshape(db, nih * ds, idx_dim)
    iw_r = jnp.broadcast_to(iw[sl].reshape(db, ds, nih).transpose(0, 2, 1).reshape(db, nih * ds, 1),
                            (db, nih * ds, LANES))
    q_r = q[sl].reshape(db, ds, n_heads, hd).transpose(0, 2, 1, 3).reshape(db, n_heads * ds, hd)
    pad_keys = lambda a: jnp.pad(a.reshape(db, ds, -1), ((0, 0), (0, LANES - ds), (0, 0)))
    ao_s = _sample_attention(page_table, iq_r, iw_r, q_r, pad_keys(ik16[sl]), pad_keys(k16[sl]),
                             pad_keys(v16[sl]),
                             cache_ik[0], cache_k[0].reshape(n_pool, page * n_kv, hd),
                             cache_v[0].reshape(n_pool, page * n_kv, hd),
                             ds=ds, nih=nih, n_heads=n_heads, n_kv=n_kv, topk=topk_s)
    ao_s = ao_s.reshape(db, n_heads, ds, hd).transpose(0, 2, 1, 3).reshape(ms, attn_w)
    ao = jnp.concatenate([ao_p, ao_s, jnp.zeros((mp - tp - ms, attn_w), _bf16)], axis=0)

    m = _merge(ca, ao, cast(w_conv_out[0]), cast(w_o[0]), vec(b_conv_out[0]), sg)
    h2 = _resid_proj(m, cast(w_out[0]), h1)
    y = _ffn(h2, vec(norm2_g[0]), cast(ffn2_w1[0]), cast(ffn2_w3[0]), cast(ffn2_w2[0]),
             vec(final_g), emit_h=False)

    y_prompt = y[n_meta:t][None]
    y_sample = y[sl].reshape(db, ds, d)
    hist = width - 1
    new_conv_p = glu[t - hist:t][None, None]
    new_conv_s = jnp.concatenate([state_conv[0], glu_s], axis=1)[:, -hist:][None]
    return (y_prompt, y_sample,
            k32[:t].reshape(1, 1, t, n_kv, hd), v32[:t].reshape(1, 1, t, n_kv, hd),
            ik32[:t].reshape(1, 1, t, idx_dim), new_conv_p,
            k32[sl].reshape(1, db, ds, n_kv, hd), v32[sl].reshape(1, db, ds, n_kv, hd),
            ik32[sl].reshape(1, db, ds, idx_dim), new_conv_s)
```

```python
import functools

import numpy as np
import jax
import jax.numpy as jnp
from jax import lax
from jax.experimental import pallas as pl
from jax.experimental.pallas import tpu as pltpu

MAX_TOPK = 256
ROPE_THETA = 10000.0
EPS = 1e-6

LANES = 128
FFN_ROW_TILE = 512
FF_TILE = 512
PROJ_ROW_CAP = 1088
PROJ_COL_TILE = 1024
Q_TILE = 256
PAGES_PER_STEP_INDEX = 32
PAGES_PER_STEP_ATTEND = 8
SEARCH_UNROLL_SAMPLE = 16
VMEM_LIMIT = 56 * 1024 * 1024
INT_MIN = -(2 ** 31)
NEG = -1e30
LOG2E = 1.4426950408889634

_bf16 = jnp.bfloat16
_f32 = jnp.float32


def _cparams(sem):
    return pltpu.CompilerParams(dimension_semantics=sem, vmem_limit_bytes=VMEM_LIMIT)


def _dot(a, b):
    return jnp.dot(a, b, preferred_element_type=_f32)


def _dot_nt(a, b):
    return lax.dot_general(a, b, (((1,), (1,)), ((), ())), preferred_element_type=_f32)


def _rms(x, g):
    return x * lax.rsqrt(jnp.mean(x * x, axis=-1, keepdims=True) + EPS) * g


def _sigmoid(x):
    return 1.0 / (1.0 + jnp.exp(-x))


def _round_up(x, m):
    return (x + m - 1) // m * m


def _row_tile(rows, cap):
    return max(t for t in range(16, cap + 1, 16) if rows % t == 0)


def _col_tile(n):
    return PROJ_COL_TILE if n % PROJ_COL_TILE == 0 else (512 if n % 512 == 0 else n)


def _ffn_kernel(h_ref, g_ref, w1_ref, w3_ref, w2_ref, g2_ref, *refs, n_f, emit_h):
    if emit_h:
        h_out_ref, u_out_ref, u_sc, acc_sc = refs
    else:
        y_out_ref, u_sc, acc_sc = refs
    f = pl.program_id(1)

    @pl.when(f == 0)
    def _():
        u_sc[...] = _rms(h_ref[...], g_ref[...]).astype(_bf16)
        acc_sc[...] = jnp.zeros_like(acc_sc)

    u = u_sc[...]
    a = _dot(u, w1_ref[...])
    b = _dot(u, w3_ref[...])
    gate = (a * _sigmoid(a)) * b
    acc_sc[...] += _dot(gate.astype(_bf16), w2_ref[...])

    @pl.when(f == n_f - 1)
    def _():
        hn = h_ref[...] + 0.5 * acc_sc[...]
        un = _rms(hn, g2_ref[...])
        if emit_h:
            h_out_ref[...] = hn
            u_out_ref[...] = un.astype(_bf16)
        else:
            y_out_ref[...] = un


def _ffn(h, g, w1, w3, w2, g2, *, emit_h):
    mp, d = h.shape
    dff = w1.shape[1]
    tm = FFN_ROW_TILE
    tf = min(FF_TILE, dff)
    assert mp % tm == 0 and dff % tf == 0
    n_f = dff // tf
    row = pl.BlockSpec((tm, d), lambda i, f: (i, 0))
    vec = pl.BlockSpec((1, d), lambda i, f: (0, 0))
    if emit_h:
        out_shape = (jax.ShapeDtypeStruct((mp, d), _f32), jax.ShapeDtypeStruct((mp, d), _bf16))
        out_specs = (row, row)
    else:
        out_shape = jax.ShapeDtypeStruct((mp, d), _f32)
        out_specs = row
    return pl.pallas_call(
        functools.partial(_ffn_kernel, n_f=n_f, emit_h=emit_h),
        grid=(mp // tm, n_f),
        in_specs=[row, vec,
                  pl.BlockSpec((d, tf), lambda i, f: (0, f)),
                  pl.BlockSpec((d, tf), lambda i, f: (0, f)),
                  pl.BlockSpec((tf, d), lambda i, f: (f, 0)),
                  vec],
        out_specs=out_specs,
        out_shape=out_shape,
        scratch_shapes=[pltpu.VMEM((tm, d), _bf16), pltpu.VMEM((tm, d), _f32)],
        compiler_params=_cparams(("parallel", "arbitrary")),
        name="ffn_emit_h" if emit_h else "ffn_final",
    )(h, g, w1, w3, w2, g2)


def _glu_kernel(u_ref, wa_ref, wb_ref, o_ref):
    u = u_ref[...]
    o_ref[...] = _dot(u, wa_ref[...]) * _sigmoid(_dot(u, wb_ref[...]))


def _proj_glu(u, wa, wb):
    mp, d = u.shape
    n = wa.shape[1]
    tm, tn = _row_tile(mp, PROJ_ROW_CAP), _col_tile(n)
    return pl.pallas_call(
        _glu_kernel,
        grid=(mp // tm, n // tn),
        in_specs=[pl.BlockSpec((tm, d), lambda i, j: (i, 0)),
                  pl.BlockSpec((d, tn), lambda i, j: (0, j)),
                  pl.BlockSpec((d, tn), lambda i, j: (0, j))],
        out_specs=pl.BlockSpec((tm, tn), lambda i, j: (i, j)),
        out_shape=jax.ShapeDtypeStruct((mp, n), _f32),
        compiler_params=_cparams(("parallel", "parallel")),
        name="proj_glu",
    )(u, wa, wb)


def _proj_kernel(u_ref, w_ref, *refs, mode, scale, out_dtypes):
    if mode == "rope":
        cos_ref, sin_ref = refs[:2]
        outs = refs[2:]
    else:
        outs = refs
    z = _dot(u_ref[...], w_ref[...])
    if mode == "rope":
        cos = cos_ref[...]
        sin = sin_ref[...]
        for j in range(z.shape[1] // LANES):
            cs = slice(j * LANES, (j + 1) * LANES)
            x = z[:, cs]
            r = x * cos + pltpu.roll(x, LANES // 2, 1) * sin
            if scale != 1.0:
                r = r * scale
            for o_ref, dt in zip(outs, out_dtypes):
                o_ref[:, cs] = r.astype(dt)
        return
    if mode == "sigmoid":
        z = _sigmoid(z)
    if scale != 1.0:
        z = z * scale
    for o_ref, dt in zip(outs, out_dtypes):
        o_ref[...] = z.astype(dt)


def _proj(u, w, *, mode="plain", cos=None, sin=None, scale=1.0, out_dtypes=(_f32,), name="proj"):
    mp, d = u.shape
    n = w.shape[1]
    tm, tn = _row_tile(mp, PROJ_ROW_CAP), _col_tile(n)
    in_specs = [pl.BlockSpec((tm, d), lambda i, j: (i, 0)),
                pl.BlockSpec((d, tn), lambda i, j: (0, j))]
    args = [u, w]
    if mode == "rope":
        in_specs += [pl.BlockSpec((tm, LANES), lambda i, j: (i, 0))] * 2
        args += [cos, sin]
    ospec = pl.BlockSpec((tm, tn), lambda i, j: (i, j))
    outs = pl.pallas_call(
        functools.partial(_proj_kernel, mode=mode, scale=scale, out_dtypes=out_dtypes),
        grid=(mp // tm, n // tn),
        in_specs=in_specs,
        out_specs=tuple(ospec for _ in out_dtypes),
        out_shape=tuple(jax.ShapeDtypeStruct((mp, n), dt) for dt in out_dtypes),
        compiler_params=_cparams(("parallel", "parallel")),
        name=name,
    )(*args)
    return outs


def _ln_swish(c, g, b):
    mu = jnp.mean(c, axis=-1, keepdims=True)
    var = jnp.mean(jnp.square(c - mu), axis=-1, keepdims=True)
    y = (c - mu) * lax.rsqrt(var + EPS) * g + b
    return y * _sigmoid(y)


def _conv_prompt_kernel(cur_ref, halo_ref, w_ref, bdw_ref, g_ref, b_ref, o_ref,
                        ext_sc, sh_sc, wb_sc, c_sc, *, width, tc, halo, cblk):
    i = pl.program_id(0)
    ch = cur_ref.shape[1]
    rb = 16
    ext_sc[pl.ds(0, halo), :] = jnp.where(i > 0, halo_ref[...], 0.0)
    ext_sc[pl.ds(halo, tc), :] = cur_ref[...]
    ext_sc[pl.ds(halo + tc, 8), :] = jnp.zeros((8, ch), _f32)
    lead = halo - (width - 1)
    for cb in range(ch // cblk):
        cs = pl.ds(cb * cblk, cblk)
        for r in range(8):
            sh_sc[r] = ext_sc[pl.ds(r, halo + tc), cs]
        for j in range(width):
            wb_sc[j] = jnp.broadcast_to(w_ref[pl.ds(j, 1), cs], (8, cblk))
        bias = jnp.broadcast_to(bdw_ref[:, cs], (8, cblk))

        def rows(t, _):
            t0 = pl.multiple_of(t * rb, rb)
            accs = [bias] * (rb // 8)
            for j in range(width):
                off = lead + j
                wv = wb_sc[j]
                for a in range(rb // 8):
                    x = sh_sc[off % 8, pl.ds(t0 + a * 8 + (off // 8) * 8, 8), :]
                    accs[a] = accs[a] + x * wv
            for a in range(rb // 8):
                c_sc[pl.ds(t0 + a * 8, 8), cs] = accs[a]
            return 0

        lax.fori_loop(0, tc // rb, rows, 0)
    o_ref[...] = _ln_swish(c_sc[...], g_ref[...], b_ref[...]).astype(_bf16)


def _conv_prompt(glu, tp, w_dw, b_dw, ln_g, ln_b):
    ch = glu.shape[1]
    width = w_dw.shape[0]
    tc = Q_TILE
    halo = 32
    assert width - 1 <= halo and tp % tc == 0 and tc % halo == 0
    cblk = min(512, ch)
    vec = pl.BlockSpec((1, ch), lambda i: (0, 0))
    return pl.pallas_call(
        functools.partial(_conv_prompt_kernel, width=width, tc=tc, halo=halo, cblk=cblk),
        grid=(tp // tc,),
        in_specs=[pl.BlockSpec((tc, ch), lambda i: (i, 0)),
                  pl.BlockSpec((halo, ch), lambda i: (jnp.maximum(i * (tc // halo) - 1, 0), 0)),
                  pl.BlockSpec((width, ch), lambda i: (0, 0)),
                  vec, vec, vec],
        out_specs=pl.BlockSpec((tc, ch), lambda i: (i, 0)),
        out_shape=jax.ShapeDtypeStruct((tp, ch), _bf16),
        scratch_shapes=[pltpu.VMEM((halo + tc + 8, ch), _f32),
                        pltpu.VMEM((8, halo + tc, cblk), _f32),
                        pltpu.VMEM((width, 8, cblk), _f32),
                        pltpu.VMEM((tc, ch), _f32)],
        compiler_params=_cparams(("parallel",)),
        name="conv_prompt",
    )(glu, glu, w_dw, b_dw, ln_g, ln_b)


def _conv_sample_kernel(ext_ref, w_ref, bdw_ref, g_ref, b_ref, o_ref, *, width, ds):
    ch = ext_ref.shape[2]
    acc = jnp.broadcast_to(bdw_ref[...], (ds, ch))
    for j in range(width):
        acc = acc + ext_ref[0, pl.ds(j, ds), :] * w_ref[pl.ds(j, 1), :]
    o_ref[0] = _ln_swish(acc, g_ref[...], b_ref[...]).astype(_bf16)


def _conv_sample(ext, w_dw, b_dw, ln_g, ln_b, ds):
    db, rows, ch = ext.shape
    width = w_dw.shape[0]
    vec = pl.BlockSpec((1, ch), lambda b: (0, 0))
    return pl.pallas_call(
        functools.partial(_conv_sample_kernel, width=width, ds=ds),
        grid=(db,),
        in_specs=[pl.BlockSpec((1, rows, ch), lambda b: (b, 0, 0)),
                  pl.BlockSpec((width, ch), lambda b: (0, 0)),
                  vec, vec, vec],
        out_specs=pl.BlockSpec((1, ds, ch), lambda b: (b, 0, 0)),
        out_shape=jax.ShapeDtypeStruct((db, ds, ch), _bf16),
        compiler_params=_cparams(("parallel",)),
        name="conv_sample",
    )(ext, w_dw, b_dw, ln_g, ln_b)


def _merge_kernel(ca_ref, ao_ref, wco_ref, wo_ref, bco_ref, sgc_ref, sga_ref, m_ref):
    yc = _dot(ca_ref[...], wco_ref[...]) + bco_ref[...]
    ya = _dot(ao_ref[...], wo_ref[...])
    m_ref[...] = (sgc_ref[...].astype(_f32) * yc + sga_ref[...].astype(_f32) * ya).astype(_bf16)


def _merge(ca, ao, wco, wo, bco, sg):
    mp, ch = ca.shape
    aw = ao.shape[1]
    d = wco.shape[1]
    tm = _row_tile(mp, PROJ_ROW_CAP)
    tn = min(512, d)
    nj = d // tn
    return pl.pallas_call(
        _merge_kernel,
        grid=(mp // tm, nj),
        in_specs=[pl.BlockSpec((tm, ch), lambda i, j: (i, 0)),
                  pl.BlockSpec((tm, aw), lambda i, j: (i, 0)),
                  pl.BlockSpec((ch, tn), lambda i, j: (0, j)),
                  pl.BlockSpec((aw, tn), lambda i, j: (0, j)),
                  pl.BlockSpec((1, tn), lambda i, j: (0, j)),
                  pl.BlockSpec((tm, tn), lambda i, j: (i, j)),
                  pl.BlockSpec((tm, tn), lambda i, j: (i, j + nj))],
        out_specs=pl.BlockSpec((tm, tn), lambda i, j: (i, j)),
        out_shape=jax.ShapeDtypeStruct((mp, d), _bf16),
        compiler_params=_cparams(("parallel", "parallel")),
        name="merge",
    )(ca, ao, wco, wo, bco, sg, sg)


def _resid_kernel(m_ref, w_ref, h_ref, o_ref):
    o_ref[...] = h_ref[...] + _dot(m_ref[...], w_ref[...])


def _resid_proj(m, w, h):
    mp, d = m.shape
    n = w.shape[1]
    tm, tn = _row_tile(mp, PROJ_ROW_CAP), _col_tile(n)
    return pl.pallas_call(
        _resid_kernel,
        grid=(mp // tm, n // tn),
        in_specs=[pl.BlockSpec((tm, d), lambda i, j: (i, 0)),
                  pl.BlockSpec((d, tn), lambda i, j: (0, j)),
                  pl.BlockSpec((tm, tn), lambda i, j: (i, j))],
        out_specs=pl.BlockSpec((tm, tn), lambda i, j: (i, j)),
        out_shape=jax.ShapeDtypeStruct((mp, n), _f32),
        compiler_params=_cparams(("parallel", "parallel")),
        name="resid_proj",
    )(m, w, h)


def _order_key(s):
    b = lax.bitcast_convert_type(s, jnp.int32)
    return b ^ (lax.shift_right_arithmetic(b, 31) & jnp.int32(0x7FFFFFFF))


def _bit_search(count_ge, rows, n_bits, k, signed):
    flip = jnp.int32(INT_MIN) if signed else jnp.int32(0)

    def body(b, t_u):
        bit = lax.shift_left(jnp.int32(1), jnp.int32(n_bits - 1) - jnp.asarray(b, jnp.int32))
        cand = t_u | bit
        cnt = count_ge(cand ^ flip)
        return jnp.where(cnt >= k, cand, t_u)

    t_u = lax.fori_loop(0, n_bits, body, jnp.zeros((rows, 1), jnp.int32))
    return t_u ^ flip


def _select_threshold(load, store, n_iters, unroll, rows, width, k, idx_bits):
    halves = width // LANES

    def scan(fn, init):
        def body(it, carry):
            for u in range(unroll):
                carry = fn(it * unroll + u, carry)
            return carry
        return lax.fori_loop(0, n_iters, body, init)

    def count(pred):
        def one(c, acc):
            m = pred(load(c), c)
            for j in range(halves):
                acc = acc + jnp.where(m[:, j * LANES:(j + 1) * LANES], 1.0, 0.0)
            return acc
        return jnp.sum(scan(one, jnp.zeros((rows, LANES), _f32)), axis=1, keepdims=True)

    kf = jnp.float32(k)
    t = _bit_search(lambda th: count(lambda x, c: x >= th), rows, 32, kf, True)
    t = jnp.maximum(t, jnp.int32(INT_MIN + 1))
    n_ge = count(lambda x, c: x >= t)

    @pl.when(jnp.max(n_ge) > kf)
    def _():
        n_gt = count(lambda x, c: x > t)
        need = kf - n_gt
        lane = lax.broadcasted_iota(jnp.int32, (rows, width), 1)

        def ties_below(j):
            return count(lambda x, c: (x == t) & ((lane + c * width) < j))

        j_cut = _bit_search(lambda j: need - ties_below(j) + (kf - 1.0), rows, idx_bits, kf, False)
        surplus = n_ge > kf

        def rewrite(c, carry):
            x = load(c)
            drop = (x == t) & ((lane + c * width) > j_cut) & surplus
            store(c, jnp.where(drop, jnp.int32(INT_MIN), x))
            return carry
        scan(rewrite, 0)

    return t


def _prompt_attn_kernel(iq_ref, iw_ref, q_ref, ikt_ref, kt_ref, v_ref, o_ref,
                        key_sc, thr_sc, wb_sc, m_sc, acc_sc, *, nih, n_heads, n_kv, topk, idx_bits, sub):
    i = pl.program_id(0)
    tq = q_ref.shape[0]
    group = n_heads // n_kv
    unroll = 2

    for h in range(nih):
        wb_sc[h] = jnp.broadcast_to(iw_ref[:, h:h + 1], (tq, LANES))

    def scores(c):
        rhs = ikt_ref[c]
        halves = [jnp.zeros((tq, LANES), _f32) for _ in range(tq // LANES)]
        for h in range(nih):
            d = _dot(iq_ref[:, h * LANES:(h + 1) * LANES], rhs)
            w = wb_sc[h]
            for j in range(len(halves)):
                halves[j] = halves[j] + jnp.maximum(d[:, j * LANES:(j + 1) * LANES], 0.0) * w
        return _order_key(jnp.concatenate(halves, axis=1))

    def score_body(c, _):
        key_sc[c] = scores(c)
        return 0

    lax.fori_loop(0, i, score_body, 0)
    row = lax.broadcasted_iota(jnp.int32, (tq, tq), 0)
    col = lax.broadcasted_iota(jnp.int32, (tq, tq), 1)
    key_sc[i] = jnp.where(col <= row, scores(i), jnp.int32(INT_MIN))
    key_sc[i + 1] = jnp.full((tq, tq), INT_MIN, jnp.int32)
    n_iters = lax.shift_right_logical(i + unroll, 1)

    for rb in range(tq // sub):
        rs = pl.ds(rb * sub, sub)
        t = _select_threshold(lambda c: key_sc[c, rs, :],
                              lambda c, x: key_sc.__setitem__((c, rs, slice(None)), x),
                              n_iters, unroll, sub, tq, topk, idx_bits)
        thr_sc[rs, :] = jnp.broadcast_to(t, (sub, LANES))

    thr = jnp.concatenate([thr_sc[...]] * (tq // LANES), axis=1)
    m_sc[...] = jnp.full(m_sc.shape, NEG, _f32)
    acc_sc[...] = jnp.zeros_like(acc_sc)
    ones = jnp.ones((tq, LANES), _bf16)

    def attend(c, _):
        bias = jnp.where(key_sc[c] >= thr, 0.0, NEG)
        vc = v_ref[c]

        for n in range(n_kv):
            kt = kt_ref[c, n]
            vn1 = jnp.concatenate([vc[:, n * LANES:(n + 1) * LANES], ones], axis=1)
            for g in range(group):
                h = n * group + g
                lg = _dot(q_ref[:, h * LANES:(h + 1) * LANES], kt) + bias
                m_prev = m_sc[h]
                m_new = jnp.maximum(m_prev, jnp.max(lg, axis=1, keepdims=True))
                alpha = jnp.exp2(m_prev - m_new)
                p = jnp.exp2(lg - jnp.concatenate([m_new] * (tq // LANES), axis=1))
                acc_sc[h] = jnp.concatenate([alpha, alpha], axis=1) * acc_sc[h] + _dot(p.astype(_bf16), vn1)
                m_sc[h] = m_new
        return 0

    lax.fori_loop(0, i + 1, attend, 0)
    for h in range(n_heads):
        a = acc_sc[h]
        o_ref[:, h * LANES:(h + 1) * LANES] = (a[:, :LANES] / a[:, LANES:]).astype(_bf16)


def _prompt_attention(iq, iw, q, ikt, kt, v, *, tp, nih, n_heads, n_kv, topk):
    tq = Q_TILE
    nc = tp // tq
    resident = lambda shape: pl.BlockSpec(shape, lambda i: (0,) * len(shape),
                                          pipeline_mode=pl.Buffered(1))
    idx_bits = max(1, int(np.ceil(np.log2(tp + 1))))
    return pl.pallas_call(
        functools.partial(_prompt_attn_kernel, nih=nih, n_heads=n_heads, n_kv=n_kv, topk=topk,
                          idx_bits=idx_bits, sub=min(64, tq)),
        grid=(nc,),
        in_specs=[pl.BlockSpec((tq, nih * LANES), lambda i: (i, 0)),
                  pl.BlockSpec((tq, nih), lambda i: (i, 0)),
                  pl.BlockSpec((tq, n_heads * LANES), lambda i: (i, 0)),
                  resident((nc, LANES, tq)),
                  resident((nc, n_kv, LANES, tq)),
                  resident((nc, tq, n_kv * LANES))],
        out_specs=pl.BlockSpec((tq, n_heads * LANES), lambda i: (i, 0)),
        out_shape=jax.ShapeDtypeStruct((tp, n_heads * LANES), _bf16),
        scratch_shapes=[pltpu.VMEM((nc + 1, tq, tq), jnp.int32),
                        pltpu.VMEM((tq, LANES), jnp.int32),
                        pltpu.VMEM((nih, tq, LANES), _f32),
                        pltpu.VMEM((n_heads, tq, LANES), _f32),
                        pltpu.VMEM((n_heads, tq, 2 * LANES), _f32)],
        compiler_params=_cparams(("arbitrary",)),
        name="prompt_attention",
    )(iq, iw, q, ikt, kt, v)


def _sample_index_kernel(pt_ref, iq_ref, iw_ref, ikn_ref, *refs,
                         pps, n_steps, n_pages, n_chunks, ds, nih, topk, idx_bits):
    ik_refs = refs[:pps]
    key_ref, thr_ref = refs[pps:]
    s = pl.program_id(1)

    def index_keys(ik_page):
        x = jnp.maximum(_dot_nt(iq_ref[0], ik_page), 0.0) * iw_ref[0]
        return _order_key(jnp.sum(x.reshape(nih, ds, LANES), axis=0))

    for j in range(pps):
        key_ref[0, s * pps + j] = index_keys(ik_refs[j][0].astype(_bf16))

    @pl.when(s == n_steps - 1)
    def _():
        qi = lax.broadcasted_iota(jnp.int32, (ds, LANES), 0)
        ki = lax.broadcasted_iota(jnp.int32, (ds, LANES), 1)
        key_ref[0, n_pages] = jnp.where(ki <= qi, index_keys(ikn_ref[0]), jnp.int32(INT_MIN))
        for c in range(n_pages + 1, n_chunks):
            key_ref[0, c] = jnp.full((ds, LANES), INT_MIN, jnp.int32)
        t = _select_threshold(lambda c: key_ref[0, c],
                              lambda c, x: key_ref.__setitem__((0, c), x),
                              n_chunks // SEARCH_UNROLL_SAMPLE, SEARCH_UNROLL_SAMPLE,
                              ds, LANES, topk, idx_bits)
        thr_ref[0] = jnp.broadcast_to(t, (ds, LANES))


def _sample_index(page_table, iq_r, iw_r, ik_new, cache_ik, *, ds, nih, topk):
    db, n_pages = page_table.shape
    pps = min(PAGES_PER_STEP_INDEX, n_pages)
    assert n_pages % pps == 0
    n_steps = n_pages // pps
    page = cache_ik.shape[1]
    n_chunks = _round_up(n_pages + 1, SEARCH_UNROLL_SAMPLE)
    rows_i = nih * ds
    idx_bits = max(1, int(np.ceil(np.log2(n_chunks * LANES + 1))))
    per_b = lambda shape: pl.BlockSpec((1,) + shape, lambda b, s, pt: (b,) + (0,) * len(shape))

    def ik_spec(j):
        return pl.BlockSpec((1, page, LANES), lambda b, s, pt: (pt[b * n_pages + s * pps + j], 0, 0))

    grid_spec = pltpu.PrefetchScalarGridSpec(
        num_scalar_prefetch=1,
        grid=(db, n_steps),
        in_specs=[per_b((rows_i, LANES)), per_b((rows_i, LANES)), per_b((LANES, LANES))]
                 + [ik_spec(j) for j in range(pps)],
        out_specs=(per_b((n_chunks, ds, LANES)), per_b((ds, LANES))),
    )
    return pl.pallas_call(
        functools.partial(_sample_index_kernel, pps=pps, n_steps=n_steps, n_pages=n_pages,
                          n_chunks=n_chunks, ds=ds, nih=nih, topk=topk, idx_bits=idx_bits),
        grid_spec=grid_spec,
        out_shape=(jax.ShapeDtypeStruct((db, n_chunks, ds, LANES), jnp.int32),
                   jax.ShapeDtypeStruct((db, ds, LANES), jnp.int32)),
        compiler_params=_cparams(("parallel", "arbitrary")),
        name="sample_index",
    )(page_table.reshape(-1), iq_r, iw_r, ik_new, *([cache_ik] * pps))


def _sample_attend_kernel(pt_ref, q_ref, key_ref, thr_ref, kn_ref, vn_ref, *refs,
                          pps, n_steps, n_pages, ds, n_heads, n_kv):
    k_refs = refs[:pps]
    v_refs = refs[pps:2 * pps]
    o_ref = refs[2 * pps]
    m_sc, l_sc, acc_sc = refs[2 * pps + 1:]
    s = pl.program_id(1)
    group = n_heads // n_kv
    rows_g = group * ds
    page = k_refs[0].shape[1] // n_kv

    @pl.when(s == 0)
    def _():
        m_sc[...] = jnp.full(m_sc.shape, NEG, _f32)
        l_sc[...] = jnp.zeros_like(l_sc)
        acc_sc[...] = jnp.zeros_like(acc_sc)

    def attend(chunk_ids, k_of, v_of):
        thr = thr_ref[0]
        bias = jnp.concatenate([jnp.where(key_ref[0, c] >= thr, 0.0, NEG) for c in chunk_ids], axis=1)
        bias = jnp.concatenate([bias] * group, axis=0)
        m_prev, l_prev, acc_prev = m_sc[...], l_sc[...], acc_sc[...]
        heads = [slice(n * rows_g, (n + 1) * rows_g) for n in range(n_kv)]
        lgs = [_dot_nt(q_ref[0, rs, :],
                       jnp.concatenate([k_of(j, n) for j in range(len(chunk_ids))], axis=0)) + bias
               for n, rs in enumerate(heads)]
        m_out = [jnp.maximum(m_prev[rs], jnp.max(lg, axis=1, keepdims=True)) for lg, rs in zip(lgs, heads)]
        ps = [jnp.exp2(lg - m_new[:, :1]) for lg, m_new in zip(lgs, m_out)]
        pvs = [_dot(p.astype(_bf16), jnp.concatenate([v_of(j, n) for j in range(len(chunk_ids))], axis=0))
               for n, p in enumerate(ps)]
        l_out, acc_out = [], []
        for rs, m_new, p, pv in zip(heads, m_out, ps, pvs):
            alpha = jnp.exp2(m_prev[rs] - m_new)
            l_out.append(alpha * l_prev[rs] + jnp.sum(p, axis=1, keepdims=True))
            acc_out.append(alpha * acc_prev[rs] + pv)
        m_sc[...] = jnp.concatenate(m_out, axis=0)
        l_sc[...] = jnp.concatenate(l_out, axis=0)
        acc_sc[...] = jnp.concatenate(acc_out, axis=0)

    def head_rows(ref, n):
        return ref[0, pl.ds(n, page, stride=n_kv), :].astype(_bf16)

    attend([s * pps + j for j in range(pps)],
           lambda j, n: head_rows(k_refs[j], n), lambda j, n: head_rows(v_refs[j], n))

    @pl.when(s == n_steps - 1)
    def _():
        attend([n_pages],
               lambda j, n: kn_ref[0, :, n * LANES:(n + 1) * LANES],
               lambda j, n: vn_ref[0, :, n * LANES:(n + 1) * LANES])
        o_ref[0] = (acc_sc[...] / l_sc[...]).astype(_bf16)


def _sample_attend(page_table, q_r, keys, thr, k_new, v_new, cache_k, cache_v, *, ds, n_heads, n_kv):
    db, n_pages = page_table.shape
    pps = min(PAGES_PER_STEP_ATTEND, n_pages)
    assert n_pages % pps == 0
    n_steps = n_pages // pps
    n_chunks = keys.shape[1]
    rows_q = n_heads * ds
    kvw = n_kv * LANES
    page_rows = cache_k.shape[1]
    per_b = lambda shape: pl.BlockSpec((1,) + shape, lambda b, s, pt: (b,) + (0,) * len(shape))

    def kv_spec(j):
        return pl.BlockSpec((1, page_rows, LANES), lambda b, s, pt: (pt[b * n_pages + s * pps + j], 0, 0))

    grid_spec = pltpu.PrefetchScalarGridSpec(
        num_scalar_prefetch=1,
        grid=(db, n_steps),
        in_specs=[per_b((rows_q, LANES)), per_b((n_chunks, ds, LANES)), per_b((ds, LANES)),
                  per_b((LANES, kvw)), per_b((LANES, kvw))]
                 + [kv_spec(j) for j in range(pps)] * 2,
        out_specs=per_b((rows_q, LANES)),
        scratch_shapes=[pltpu.VMEM((rows_q, LANES), _f32),
                        pltpu.VMEM((rows_q, LANES), _f32),
                        pltpu.VMEM((rows_q, LANES), _f32)],
    )
    return pl.pallas_call(
        functools.partial(_sample_attend_kernel, pps=pps, n_steps=n_steps, n_pages=n_pages, ds=ds,
                          n_heads=n_heads, n_kv=n_kv),
        grid_spec=grid_spec,
        out_shape=jax.ShapeDtypeStruct((db, rows_q, LANES), _bf16),
        compiler_params=_cparams(("parallel", "arbitrary")),
        name="sample_attend",
    )(page_table.reshape(-1), q_r, keys, thr, k_new, v_new, *([cache_k] * pps), *([cache_v] * pps))


def _rope_tables(pos, dim):
    half = dim // 2
    inv = ROPE_THETA ** (-jnp.arange(half, dtype=_f32) / half)
    ang = pos.astype(_f32)[:, None] * inv[None, :]
    cos, sin = jnp.cos(ang), jnp.sin(ang)
    return jnp.concatenate([cos, cos], axis=1), jnp.concatenate([-sin, sin], axis=1)


def kernel(x_prompt, x_sample, cache_k, cache_v, cache_ik, state_conv, page_table, meta,
           norm1_g, ffn1_w1, ffn1_w3, ffn1_w2, norm_mix_g, w_in, w_dw, b_dw, conv_ln_g, conv_ln_b,
           w_conv_out, b_conv_out, w_o, w_out, norm2_g, ffn2_w1, ffn2_w3, ffn2_w2, final_g):
    bsz, seq, d = x_prompt.shape
    db, ds, _ = x_sample.shape
    depth, n_pool, page, n_kv, hd = cache_k.shape
    idx_dim = cache_ik.shape[-1]
    n_meta = meta.shape[0]
    width = w_dw.shape[1]
    ch = w_dw.shape[2]
    attn_w = w_o.shape[1]
    n_heads = attn_w // hd
    kv_w = n_kv * hd
    in_w = w_in.shape[2]
    nih = (in_w - 2 * ch - attn_w - 2 * kv_w - idx_dim - 2 * d) // (idx_dim + 1)
    n_pages = page_table.shape[1]
    past = n_pages * page
    assert bsz == 1 and depth == 1 and hd == LANES and idx_dim == LANES and page == LANES
    assert 2 * ch + attn_w + 2 * kv_w + nih * idx_dim + idx_dim + nih + 2 * d == in_w
    assert (n_heads * ds) % 8 == 0 and ds % 8 == 0 and ds <= LANES

    t = n_meta + seq
    tp = _round_up(t, Q_TILE)
    ms = db * ds
    mp = _round_up(tp + ms, FFN_ROW_TILE)
    topk_p = min(MAX_TOPK, seq // 4)
    topk_s = min(MAX_TOPK, (past + ds) // 4)
    assert topk_p <= Q_TILE and topk_s <= LANES * (n_pages + 1)

    h0 = jnp.concatenate([meta, x_prompt[0], jnp.zeros((tp - t, d), _f32),
                          x_sample.reshape(ms, d), jnp.zeros((mp - tp - ms, d), _f32)], axis=0)
    pos = jnp.concatenate([jnp.arange(t, dtype=jnp.int32), jnp.zeros((tp - t,), jnp.int32),
                           jnp.tile(past + jnp.arange(ds, dtype=jnp.int32), db),
                           jnp.zeros((mp - tp - ms,), jnp.int32)])
    cos, sin = _rope_tables(pos, hd)

    vec = lambda a: a.reshape(1, -1).astype(_f32)
    cast = lambda a: a.astype(_bf16)
    offs = np.cumsum([0, ch, ch, attn_w, kv_w, kv_w, nih * idx_dim, idx_dim, nih, d, d])
    wi = w_in[0]
    cols = lambda a, b: cast(wi[:, offs[a]:offs[b]])

    h1, u = _ffn(h0, vec(norm1_g[0]), cast(ffn1_w1[0]), cast(ffn1_w3[0]), cast(ffn1_w2[0]),
                 vec(norm_mix_g[0]), emit_h=True)

    glu = _proj_glu(u, cols(0, 1), cols(1, 2))
    (q,) = _proj(u, cols(2, 3), mode="rope", cos=cos, sin=sin, scale=hd ** -0.5 * LOG2E,
                 out_dtypes=(_bf16,), name="proj_q")
    k32, k16 = _proj(u, cols(3, 4), mode="rope", cos=cos, sin=sin, out_dtypes=(_f32, _bf16), name="proj_k")
    v32, v16 = _proj(u, cols(4, 5), out_dtypes=(_f32, _bf16), name="proj_v")
    (iq,) = _proj(u, cols(5, 6), mode="rope", cos=cos, sin=sin, out_dtypes=(_bf16,), name="proj_iq")
    ik32, ik16 = _proj(u, cols(6, 7), mode="rope", cos=cos, sin=sin, out_dtypes=(_f32, _bf16), name="proj_ik")
    (iw,) = _proj(u, cols(7, 8), scale=(idx_dim ** -0.5) * (nih ** -0.5), name="proj_iw")
    (sg,) = _proj(u, cols(8, 10), mode="sigmoid", out_dtypes=(_bf16,), name="proj_gates")

    ca_p = _conv_prompt(glu, tp, w_dw[0], vec(b_dw[0]), vec(conv_ln_g[0]), vec(conv_ln_b[0]))
    glu_s = glu[tp:tp + ms].reshape(db, ds, ch)
    ext_rows = _round_up(width - 1 + ds, 8)
    ext_s = jnp.concatenate([state_conv[0], glu_s,
                             jnp.zeros((db, ext_rows - (width - 1 + ds), ch), _f32)], axis=1)
    ca_s = _conv_sample(ext_s, w_dw[0], vec(b_dw[0]), vec(conv_ln_g[0]), vec(conv_ln_b[0]), ds)
    ca = jnp.concatenate([ca_p, ca_s.reshape(ms, ch), jnp.zeros((mp - tp - ms, ch), _bf16)], axis=0)

    nc = tp // Q_TILE
    ikt = ik16[:tp].reshape(nc, Q_TILE, idx_dim).transpose(0, 2, 1)
    kt = k16[:tp].reshape(nc, Q_TILE, n_kv, hd).transpose(0, 2, 3, 1)
    vch = v16[:tp].reshape(nc, Q_TILE, kv_w)
    ao_p = _prompt_attention(iq, iw, q, ikt, kt, vch, tp=tp, nih=nih, n_heads=n_heads, n_kv=n_kv,
                             topk=topk_p)

    sl = slice(tp, tp + ms)
    iq_r = iq[sl].reshape(db, ds, nih, idx_dim).transpose(0, 2, 1, 3).reshape(db, nih * ds, idx_dim)
    iw_r = jnp.broadcast_to(iw[sl].reshape(db, ds, nih).transpose(0, 2, 1).reshape(db, nih * ds, 1),
                            (db, nih * ds, LANES))
    q_r = q[sl].reshape(db, ds, n_heads, hd).transpose(0, 2, 1, 3).reshape(db, n_heads * ds, hd)
    pad_keys = lambda a: jnp.pad(a.reshape(db, ds, -1), ((0, 0), (0, LANES - ds), (0, 0)))
    keys_s, thr_s = _sample_index(page_table, iq_r, iw_r, pad_keys(ik16[sl]), cache_ik[0],
                                  ds=ds, nih=nih, topk=topk_s)
    ao_s = _sample_attend(page_table, q_r, keys_s, thr_s, pad_keys(k16[sl]), pad_keys(v16[sl]),
                          cache_k[0].reshape(n_pool, page * n_kv, hd),
                          cache_v[0].reshape(n_pool, page * n_kv, hd),
                          ds=ds, n_heads=n_heads, n_kv=n_kv)
    ao_s = ao_s.reshape(db, n_heads, ds, hd).transpose(0, 2, 1, 3).reshape(ms, attn_w)
    ao = jnp.concatenate([ao_p, ao_s, jnp.zeros((mp - tp - ms, attn_w), _bf16)], axis=0)

    m = _merge(ca, ao, cast(w_conv_out[0]), cast(w_o[0]), vec(b_conv_out[0]), sg)
    h2 = _resid_proj(m, cast(w_out[0]), h1)
    y = _ffn(h2, vec(norm2_g[0]), cast(ffn2_w1[0]), cast(ffn2_w3[0]), cast(ffn2_w2[0]),
             vec(final_g), emit_h=False)

    y_prompt = y[n_meta:t][None]
    y_sample = y[sl].reshape(db, ds, d)
    hist = width - 1
    new_conv_p = glu[t - hist:t][None, None]
    new_conv_s = jnp.concatenate([state_conv[0], glu_s], axis=1)[:, -hist:][None]
    return (y_prompt, y_sample,
            k32[:t].reshape(1, 1, t, n_kv, hd), v32[:t].reshape(1, 1, t, n_kv, hd),
            ik32[:t].reshape(1, 1, t, idx_dim), new_conv_p,
            k32[sl].reshape(1, db, ds, n_kv, hd), v32[sl].reshape(1, db, ds, n_kv, hd),
            ik32[sl].reshape(1, db, ds, idx_dim), new_conv_s)
```

```python
import functools

import numpy as np
import jax
import jax.numpy as jnp
from jax import lax
from jax.experimental import pallas as pl
from jax.experimental.pallas import tpu as pltpu

MAX_TOPK = 256
ROPE_THETA = 10000.0
EPS = 1e-6

LANES = 128
FFN_ROW_TILE = 512
FF_TILE = 512
PROJ_ROW_CAP = 1088
PROJ_COL_TILE = 1024
Q_TILE = 256
PAGES_PER_STEP_INDEX = 32
PAGES_PER_STEP_ATTEND = 8
SEARCH_UNROLL_SAMPLE = 16
SEARCH_ROWS = 128
VMEM_LIMIT = 56 * 1024 * 1024
INT_MIN = -(2 ** 31)
NEG = -1e30
LOG2E = 1.4426950408889634

_bf16 = jnp.bfloat16
_f32 = jnp.float32


def _cparams(sem):
    return pltpu.CompilerParams(dimension_semantics=sem, vmem_limit_bytes=VMEM_LIMIT)


def _dot(a, b):
    return jnp.dot(a, b, preferred_element_type=_f32)


def _dot_nt(a, b):
    return lax.dot_general(a, b, (((1,), (1,)), ((), ())), preferred_element_type=_f32)


def _rms(x, g):
    return x * lax.rsqrt(jnp.mean(x * x, axis=-1, keepdims=True) + EPS) * g


def _sigmoid(x):
    return 1.0 / (1.0 + jnp.exp(-x))


def _round_up(x, m):
    return (x + m - 1) // m * m


def _row_tile(rows, cap):
    return max(t for t in range(16, cap + 1, 16) if rows % t == 0)


def _col_tile(n):
    return PROJ_COL_TILE if n % PROJ_COL_TILE == 0 else (512 if n % 512 == 0 else n)


def _ffn_kernel(h_ref, g_ref, w1_ref, w3_ref, w2_ref, g2_ref, *refs, n_f, emit_h):
    if emit_h:
        h_out_ref, u_out_ref, u_sc, acc_sc = refs
    else:
        y_out_ref, u_sc, acc_sc = refs
    f = pl.program_id(1)

    @pl.when(f == 0)
    def _():
        u_sc[...] = _rms(h_ref[...], g_ref[...]).astype(_bf16)
        acc_sc[...] = jnp.zeros_like(acc_sc)

    u = u_sc[...]
    a = _dot(u, w1_ref[...])
    b = _dot(u, w3_ref[...])
    gate = (a * _sigmoid(a)) * b
    acc_sc[...] += _dot(gate.astype(_bf16), w2_ref[...])

    @pl.when(f == n_f - 1)
    def _():
        hn = h_ref[...] + 0.5 * acc_sc[...]
        un = _rms(hn, g2_ref[...])
        if emit_h:
            h_out_ref[...] = hn
            u_out_ref[...] = un.astype(_bf16)
        else:
            y_out_ref[...] = un


def _ffn(h, g, w1, w3, w2, g2, *, emit_h):
    mp, d = h.shape
    dff = w1.shape[1]
    tm = FFN_ROW_TILE
    tf = min(FF_TILE, dff)
    assert mp % tm == 0 and dff % tf == 0
    n_f = dff // tf
    row = pl.BlockSpec((tm, d), lambda i, f: (i, 0))
    vec = pl.BlockSpec((1, d), lambda i, f: (0, 0))
    if emit_h:
        out_shape = (jax.ShapeDtypeStruct((mp, d), _f32), jax.ShapeDtypeStruct((mp, d), _bf16))
        out_specs = (row, row)
    else:
        out_shape = jax.ShapeDtypeStruct((mp, d), _f32)
        out_specs = row
    return pl.pallas_call(
        functools.partial(_ffn_kernel, n_f=n_f, emit_h=emit_h),
        grid=(mp // tm, n_f),
        in_specs=[row, vec,
                  pl.BlockSpec((d, tf), lambda i, f: (0, f)),
                  pl.BlockSpec((d, tf), lambda i, f: (0, f)),
                  pl.BlockSpec((tf, d), lambda i, f: (f, 0)),
                  vec],
        out_specs=out_specs,
        out_shape=out_shape,
        scratch_shapes=[pltpu.VMEM((tm, d), _bf16), pltpu.VMEM((tm, d), _f32)],
        compiler_params=_cparams(("parallel", "arbitrary")),
        name="ffn_emit_h" if emit_h else "ffn_final",
    )(h, g, w1, w3, w2, g2)


def _glu_kernel(u_ref, wa_ref, wb_ref, o_ref):
    u = u_ref[...]
    o_ref[...] = _dot(u, wa_ref[...]) * _sigmoid(_dot(u, wb_ref[...]))


def _proj_glu(u, wa, wb):
    mp, d = u.shape
    n = wa.shape[1]
    tm, tn = _row_tile(mp, PROJ_ROW_CAP), _col_tile(n)
    return pl.pallas_call(
        _glu_kernel,
        grid=(mp // tm, n // tn),
        in_specs=[pl.BlockSpec((tm, d), lambda i, j: (i, 0)),
                  pl.BlockSpec((d, tn), lambda i, j: (0, j)),
                  pl.BlockSpec((d, tn), lambda i, j: (0, j))],
        out_specs=pl.BlockSpec((tm, tn), lambda i, j: (i, j)),
        out_shape=jax.ShapeDtypeStruct((mp, n), _f32),
        compiler_params=_cparams(("parallel", "parallel")),
        name="proj_glu",
    )(u, wa, wb)


def _proj_kernel(u_ref, w_ref, *refs, mode, scale, out_dtypes):
    if mode == "rope":
        cos_ref, sin_ref = refs[:2]
        outs = refs[2:]
    else:
        outs = refs
    z = _dot(u_ref[...], w_ref[...])
    if mode == "rope":
        cos = cos_ref[...]
        sin = sin_ref[...]
        for j in range(z.shape[1] // LANES):
            cs = slice(j * LANES, (j + 1) * LANES)
            x = z[:, cs]
            r = x * cos + pltpu.roll(x, LANES // 2, 1) * sin
            if scale != 1.0:
                r = r * scale
            for o_ref, dt in zip(outs, out_dtypes):
                o_ref[:, cs] = r.astype(dt)
        return
    if mode == "sigmoid":
        z = _sigmoid(z)
    if scale != 1.0:
        z = z * scale
    for o_ref, dt in zip(outs, out_dtypes):
        o_ref[...] = z.astype(dt)


def _proj(u, w, *, mode="plain", cos=None, sin=None, scale=1.0, out_dtypes=(_f32,), name="proj"):
    mp, d = u.shape
    n = w.shape[1]
    tm, tn = _row_tile(mp, PROJ_ROW_CAP), _col_tile(n)
    in_specs = [pl.BlockSpec((tm, d), lambda i, j: (i, 0)),
                pl.BlockSpec((d, tn), lambda i, j: (0, j))]
    args = [u, w]
    if mode == "rope":
        in_specs += [pl.BlockSpec((tm, LANES), lambda i, j: (i, 0))] * 2
        args += [cos, sin]
    ospec = pl.BlockSpec((tm, tn), lambda i, j: (i, j))
    outs = pl.pallas_call(
        functools.partial(_proj_kernel, mode=mode, scale=scale, out_dtypes=out_dtypes),
        grid=(mp // tm, n // tn),
        in_specs=in_specs,
        out_specs=tuple(ospec for _ in out_dtypes),
        out_shape=tuple(jax.ShapeDtypeStruct((mp, n), dt) for dt in out_dtypes),
        compiler_params=_cparams(("parallel", "parallel")),
        name=name,
    )(*args)
    return outs


def _ln_swish(c, g, b):
    mu = jnp.mean(c, axis=-1, keepdims=True)
    var = jnp.mean(jnp.square(c - mu), axis=-1, keepdims=True)
    y = (c - mu) * lax.rsqrt(var + EPS) * g + b
    return y * _sigmoid(y)


def _conv_prompt_kernel(cur_ref, halo_ref, w_ref, bdw_ref, g_ref, b_ref, o_ref,
                        ext_sc, sh_sc, wb_sc, c_sc, *, width, tc, halo, cblk):
    i = pl.program_id(0)
    ch = cur_ref.shape[1]
    rb = 16
    ext_sc[pl.ds(0, halo), :] = jnp.where(i > 0, halo_ref[...], 0.0)
    ext_sc[pl.ds(halo, tc), :] = cur_ref[...]
    ext_sc[pl.ds(halo + tc, 8), :] = jnp.zeros((8, ch), _f32)
    lead = halo - (width - 1)
    for cb in range(ch // cblk):
        cs = pl.ds(cb * cblk, cblk)
        for r in range(8):
            sh_sc[r] = ext_sc[pl.ds(r, halo + tc), cs]
        for j in range(width):
            wb_sc[j] = jnp.broadcast_to(w_ref[pl.ds(j, 1), cs], (8, cblk))
        bias = jnp.broadcast_to(bdw_ref[:, cs], (8, cblk))

        def rows(t, _):
            t0 = pl.multiple_of(t * rb, rb)
            accs = [bias] * (rb // 8)
            for j in range(width):
                off = lead + j
                wv = wb_sc[j]
                for a in range(rb // 8):
                    x = sh_sc[off % 8, pl.ds(t0 + a * 8 + (off // 8) * 8, 8), :]
                    accs[a] = accs[a] + x * wv
            for a in range(rb // 8):
                c_sc[pl.ds(t0 + a * 8, 8), cs] = accs[a]
            return 0

        lax.fori_loop(0, tc // rb, rows, 0)
    o_ref[...] = _ln_swish(c_sc[...], g_ref[...], b_ref[...]).astype(_bf16)


def _conv_prompt(glu, tp, w_dw, b_dw, ln_g, ln_b):
    ch = glu.shape[1]
    width = w_dw.shape[0]
    tc = Q_TILE
    halo = 32
    assert width - 1 <= halo and tp % tc == 0 and tc % halo == 0
    cblk = min(512, ch)
    vec = pl.BlockSpec((1, ch), lambda i: (0, 0))
    return pl.pallas_call(
        functools.partial(_conv_prompt_kernel, width=width, tc=tc, halo=halo, cblk=cblk),
        grid=(tp // tc,),
        in_specs=[pl.BlockSpec((tc, ch), lambda i: (i, 0)),
                  pl.BlockSpec((halo, ch), lambda i: (jnp.maximum(i * (tc // halo) - 1, 0), 0)),
                  pl.BlockSpec((width, ch), lambda i: (0, 0)),
                  vec, vec, vec],
        out_specs=pl.BlockSpec((tc, ch), lambda i: (i, 0)),
        out_shape=jax.ShapeDtypeStruct((tp, ch), _bf16),
        scratch_shapes=[pltpu.VMEM((halo + tc + 8, ch), _f32),
                        pltpu.VMEM((8, halo + tc, cblk), _f32),
                        pltpu.VMEM((width, 8, cblk), _f32),
                        pltpu.VMEM((tc, ch), _f32)],
        compiler_params=_cparams(("parallel",)),
        name="conv_prompt",
    )(glu, glu, w_dw, b_dw, ln_g, ln_b)


def _conv_sample_kernel(ext_ref, w_ref, bdw_ref, g_ref, b_ref, o_ref, *, width, ds):
    ch = ext_ref.shape[2]
    acc = jnp.broadcast_to(bdw_ref[...], (ds, ch))
    for j in range(width):
        acc = acc + ext_ref[0, pl.ds(j, ds), :] * w_ref[pl.ds(j, 1), :]
    o_ref[0] = _ln_swish(acc, g_ref[...], b_ref[...]).astype(_bf16)


def _conv_sample(ext, w_dw, b_dw, ln_g, ln_b, ds):
    db, rows, ch = ext.shape
    width = w_dw.shape[0]
    vec = pl.BlockSpec((1, ch), lambda b: (0, 0))
    return pl.pallas_call(
        functools.partial(_conv_sample_kernel, width=width, ds=ds),
        grid=(db,),
        in_specs=[pl.BlockSpec((1, rows, ch), lambda b: (b, 0, 0)),
                  pl.BlockSpec((width, ch), lambda b: (0, 0)),
                  vec, vec, vec],
        out_specs=pl.BlockSpec((1, ds, ch), lambda b: (b, 0, 0)),
        out_shape=jax.ShapeDtypeStruct((db, ds, ch), _bf16),
        compiler_params=_cparams(("parallel",)),
        name="conv_sample",
    )(ext, w_dw, b_dw, ln_g, ln_b)


def _merge_kernel(ca_ref, ao_ref, wco_ref, wo_ref, bco_ref, sgc_ref, sga_ref, m_ref):
    yc = _dot(ca_ref[...], wco_ref[...]) + bco_ref[...]
    ya = _dot(ao_ref[...], wo_ref[...])
    m_ref[...] = (sgc_ref[...].astype(_f32) * yc + sga_ref[...].astype(_f32) * ya).astype(_bf16)


def _merge(ca, ao, wco, wo, bco, sg):
    mp, ch = ca.shape
    aw = ao.shape[1]
    d = wco.shape[1]
    tm = _row_tile(mp, PROJ_ROW_CAP)
    tn = min(512, d)
    nj = d // tn
    return pl.pallas_call(
        _merge_kernel,
        grid=(mp // tm, nj),
        in_specs=[pl.BlockSpec((tm, ch), lambda i, j: (i, 0)),
                  pl.BlockSpec((tm, aw), lambda i, j: (i, 0)),
                  pl.BlockSpec((ch, tn), lambda i, j: (0, j)),
                  pl.BlockSpec((aw, tn), lambda i, j: (0, j)),
                  pl.BlockSpec((1, tn), lambda i, j: (0, j)),
                  pl.BlockSpec((tm, tn), lambda i, j: (i, j)),
                  pl.BlockSpec((tm, tn), lambda i, j: (i, j + nj))],
        out_specs=pl.BlockSpec((tm, tn), lambda i, j: (i, j)),
        out_shape=jax.ShapeDtypeStruct((mp, d), _bf16),
        compiler_params=_cparams(("parallel", "parallel")),
        name="merge",
    )(ca, ao, wco, wo, bco, sg, sg)


def _resid_kernel(m_ref, w_ref, h_ref, o_ref):
    o_ref[...] = h_ref[...] + _dot(m_ref[...], w_ref[...])


def _resid_proj(m, w, h):
    mp, d = m.shape
    n = w.shape[1]
    tm, tn = _row_tile(mp, PROJ_ROW_CAP), _col_tile(n)
    return pl.pallas_call(
        _resid_kernel,
        grid=(mp // tm, n // tn),
        in_specs=[pl.BlockSpec((tm, d), lambda i, j: (i, 0)),
                  pl.BlockSpec((d, tn), lambda i, j: (0, j)),
                  pl.BlockSpec((tm, tn), lambda i, j: (i, j))],
        out_specs=pl.BlockSpec((tm, tn), lambda i, j: (i, j)),
        out_shape=jax.ShapeDtypeStruct((mp, n), _f32),
        compiler_params=_cparams(("parallel", "parallel")),
        name="resid_proj",
    )(m, w, h)


def _order_key(s):
    b = lax.bitcast_convert_type(s, jnp.int32)
    return b ^ (lax.shift_right_arithmetic(b, 31) & jnp.int32(0x7FFFFFFF))


def _bit_search(count_ge, rows, n_bits, k, lowest):
    lowest = jnp.int32(lowest)

    def body(b, t_u):
        bit = lax.shift_left(jnp.int32(1), jnp.int32(n_bits - 1) - jnp.asarray(b, jnp.int32))
        cand = t_u | bit
        cnt = count_ge(cand + lowest)
        return jnp.where(cnt >= k, cand, t_u)

    t_u = lax.fori_loop(0, n_bits, body, jnp.zeros((rows, 1), jnp.int32))
    return t_u + lowest


def _kth_key_packed(load, pk_ref, scan, rows, width, k):
    halves = width // LANES
    i16 = jnp.int16
    half_min, half_max = -(2 ** 15), 2 ** 15 - 1

    def count16(th):
        th16 = th.astype(i16)

        def one(c, acc):
            w = jnp.where(pk_ref[c] >= th16, i16(1), i16(0))
            for j in range(halves):
                acc = acc + w[:, j * LANES:(j + 1) * LANES]
            return acc
        acc = scan(one, jnp.zeros((rows, LANES), i16))
        return jnp.sum(acc.astype(_f32), axis=1, keepdims=True)

    def fill_top(c, carry):
        pk_ref[c] = lax.shift_right_arithmetic(load(c), 16).astype(i16)
        return carry
    scan(fill_top, 0)
    top = _bit_search(count16, rows, 16, k, half_min)
    above = jnp.where(top >= half_max, 0.0, count16(jnp.minimum(top + 1, half_max)))

    def fill_low(c, carry):
        x = load(c)
        low = (x & jnp.int32(0xFFFF)) + half_min
        pk_ref[c] = jnp.where(lax.shift_right_arithmetic(x, 16) == top, low, half_min).astype(i16)
        return carry
    scan(fill_low, 0)
    low = _bit_search(count16, rows, 16, k - above, half_min)
    return lax.shift_left(top, 16) | (low - half_min)


def _select_threshold(load, store, n_iters, unroll, rows, width, k, idx_bits, pk_ref=None):
    halves = width // LANES

    def scan(fn, init):
        def body(it, carry):
            for u in range(unroll):
                carry = fn(it * unroll + u, carry)
            return carry
        return lax.fori_loop(0, n_iters, body, init)

    def count(pred):
        def one(c, acc):
            m = pred(load(c), c)
            for j in range(halves):
                acc = acc + jnp.where(m[:, j * LANES:(j + 1) * LANES], 1.0, 0.0)
            return acc
        return jnp.sum(scan(one, jnp.zeros((rows, LANES), _f32)), axis=1, keepdims=True)

    kf = jnp.float32(k)
    if pk_ref is None:
        t = _bit_search(lambda th: count(lambda x, c: x >= th), rows, 32, kf, INT_MIN)
    else:
        t = _kth_key_packed(load, pk_ref, scan, rows, width, kf)
    t = jnp.maximum(t, jnp.int32(INT_MIN + 1))
    n_ge = count(lambda x, c: x >= t)

    @pl.when(jnp.max(n_ge) > kf)
    def _():
        n_gt = count(lambda x, c: x > t)
        need = kf - n_gt
        lane = lax.broadcasted_iota(jnp.int32, (rows, width), 1)

        def ties_below(j):
            return count(lambda x, c: (x == t) & ((lane + c * width) < j))

        j_cut = _bit_search(lambda j: need - ties_below(j) + (kf - 1.0), rows, idx_bits, kf, 0)
        surplus = n_ge > kf

        def rewrite(c, carry):
            x = load(c)
            drop = (x == t) & ((lane + c * width) > j_cut) & surplus
            store(c, jnp.where(drop, jnp.int32(INT_MIN), x))
            return carry
        scan(rewrite, 0)

    return t


def _prompt_attn_kernel(iq_ref, iw_ref, q_ref, ikt_ref, kt_ref, v_ref, o_ref,
                        key_sc, pk_sc, thr_sc, wb_sc, m_sc, acc_sc,
                        *, nih, n_heads, n_kv, topk, idx_bits, sub):
    i = pl.program_id(0)
    tq = q_ref.shape[0]
    group = n_heads // n_kv
    unroll = 2

    for h in range(nih):
        wb_sc[h] = jnp.broadcast_to(iw_ref[:, h:h + 1], (tq, LANES))

    def scores(c):
        rhs = ikt_ref[c]
        halves = [jnp.zeros((tq, LANES), _f32) for _ in range(tq // LANES)]
        for h in range(nih):
            d = _dot(iq_ref[:, h * LANES:(h + 1) * LANES], rhs)
            w = wb_sc[h]
            for j in range(len(halves)):
                halves[j] = halves[j] + jnp.maximum(d[:, j * LANES:(j + 1) * LANES], 0.0) * w
        return _order_key(jnp.concatenate(halves, axis=1))

    def score_body(c, _):
        key_sc[c] = scores(c)
        return 0

    lax.fori_loop(0, i, score_body, 0)
    row = lax.broadcasted_iota(jnp.int32, (tq, tq), 0)
    col = lax.broadcasted_iota(jnp.int32, (tq, tq), 1)
    key_sc[i] = jnp.where(col <= row, scores(i), jnp.int32(INT_MIN))
    key_sc[i + 1] = jnp.full((tq, tq), INT_MIN, jnp.int32)
    n_iters = lax.shift_right_logical(i + unroll, 1)

    for rb in range(tq // sub):
        rs = pl.ds(rb * sub, sub)
        t = _select_threshold(lambda c: key_sc[c, rs, :],
                              lambda c, x: key_sc.__setitem__((c, rs, slice(None)), x),
                              n_iters, unroll, sub, tq, topk, idx_bits, pk_ref=pk_sc)
        thr_sc[rs, :] = jnp.broadcast_to(t, (sub, LANES))

    thr = jnp.concatenate([thr_sc[...]] * (tq // LANES), axis=1)
    m_sc[...] = jnp.full(m_sc.shape, NEG, _f32)
    acc_sc[...] = jnp.zeros_like(acc_sc)
    ones = jnp.ones((tq, LANES), _bf16)

    def attend(c, _):
        bias = jnp.where(key_sc[c] >= thr, 0.0, NEG)
        vc = v_ref[c]

        for n in range(n_kv):
            kt = kt_ref[c, n]
            vn1 = jnp.concatenate([vc[:, n * LANES:(n + 1) * LANES], ones], axis=1)
            for g in range(group):
                h = n * group + g
                lg = _dot(q_ref[:, h * LANES:(h + 1) * LANES], kt) + bias
                m_prev = m_sc[h]
                m_new = jnp.maximum(m_prev, jnp.max(lg, axis=1, keepdims=True))
                alpha = jnp.exp2(m_prev - m_new)
                p = jnp.exp2(lg - jnp.concatenate([m_new] * (tq // LANES), axis=1))
                acc_sc[h] = jnp.concatenate([alpha, alpha], axis=1) * acc_sc[h] + _dot(p.astype(_bf16), vn1)
                m_sc[h] = m_new
        return 0

    lax.fori_loop(0, i + 1, attend, 0)
    for h in range(n_heads):
        a = acc_sc[h]
        o_ref[:, h * LANES:(h + 1) * LANES] = (a[:, :LANES] / a[:, LANES:]).astype(_bf16)


def _prompt_attention(iq, iw, q, ikt, kt, v, *, tp, nih, n_heads, n_kv, topk):
    tq = Q_TILE
    nc = tp // tq
    resident = lambda shape: pl.BlockSpec(shape, lambda i: (0,) * len(shape),
                                          pipeline_mode=pl.Buffered(1))
    idx_bits = max(1, int(np.ceil(np.log2(tp + 1))))
    sub = min(SEARCH_ROWS, tq)
    return pl.pallas_call(
        functools.partial(_prompt_attn_kernel, nih=nih, n_heads=n_heads, n_kv=n_kv, topk=topk,
                          idx_bits=idx_bits, sub=sub),
        grid=(nc,),
        in_specs=[pl.BlockSpec((tq, nih * LANES), lambda i: (i, 0)),
                  pl.BlockSpec((tq, nih), lambda i: (i, 0)),
                  pl.BlockSpec((tq, n_heads * LANES), lambda i: (i, 0)),
                  resident((nc, LANES, tq)),
                  resident((nc, n_kv, LANES, tq)),
                  resident((nc, tq, n_kv * LANES))],
        out_specs=pl.BlockSpec((tq, n_heads * LANES), lambda i: (i, 0)),
        out_shape=jax.ShapeDtypeStruct((tp, n_heads * LANES), _bf16),
        scratch_shapes=[pltpu.VMEM((nc + 1, tq, tq), jnp.int32),
                        pltpu.VMEM((nc + 1, sub, tq), jnp.int16),
                        pltpu.VMEM((tq, LANES), jnp.int32),
                        pltpu.VMEM((nih, tq, LANES), _f32),
                        pltpu.VMEM((n_heads, tq, LANES), _f32),
                        pltpu.VMEM((n_heads, tq, 2 * LANES), _f32)],
        compiler_params=_cparams(("arbitrary",)),
        name="prompt_attention",
    )(iq, iw, q, ikt, kt, v)


def _ring_copies(pt_ref, streams, sem_ref, pps, g):
    slot = lax.rem(g, 2)
    return [pltpu.make_async_copy(cache.at[pt_ref[g * pps + j]], buf.at[slot, j], sem_ref.at[slot])
            for j in range(pps) for cache, buf in streams]


def _ring_step(pt_ref, streams, sem_ref, pps, g, n_total):
    @pl.when(g == 0)
    def _():
        for cp in _ring_copies(pt_ref, streams, sem_ref, pps, g):
            cp.start()

    @pl.when(g + 1 < n_total)
    def _():
        for cp in _ring_copies(pt_ref, streams, sem_ref, pps, g + 1):
            cp.start()

    for cp in _ring_copies(pt_ref, streams, sem_ref, pps, g):
        cp.wait()


def _sample_index_kernel(pt_ref, iq_ref, iw_ref, ikn_ref, cache_ref, key_ref, thr_ref, buf, sem,
                         *, pps, n_steps, n_pages, n_chunks, ds, nih, topk, idx_bits):
    s = pl.program_id(1)
    g = pl.program_id(0) * n_steps + s
    _ring_step(pt_ref, [(cache_ref, buf)], sem, pps, g, pl.num_programs(0) * n_steps)
    slot = lax.rem(g, 2)

    def index_keys(ik_page):
        x = jnp.maximum(_dot_nt(iq_ref[0], ik_page), 0.0) * iw_ref[0]
        return _order_key(jnp.sum(x.reshape(nih, ds, LANES), axis=0))

    for j in range(pps):
        key_ref[0, s * pps + j] = index_keys(buf[slot, j].astype(_bf16))

    @pl.when(s == n_steps - 1)
    def _():
        qi = lax.broadcasted_iota(jnp.int32, (ds, LANES), 0)
        ki = lax.broadcasted_iota(jnp.int32, (ds, LANES), 1)
        key_ref[0, n_pages] = jnp.where(ki <= qi, index_keys(ikn_ref[0]), jnp.int32(INT_MIN))
        for c in range(n_pages + 1, n_chunks):
            key_ref[0, c] = jnp.full((ds, LANES), INT_MIN, jnp.int32)
        t = _select_threshold(lambda c: key_ref[0, c],
                              lambda c, x: key_ref.__setitem__((0, c), x),
                              n_chunks // SEARCH_UNROLL_SAMPLE, SEARCH_UNROLL_SAMPLE,
                              ds, LANES, topk, idx_bits)
        thr_ref[0] = jnp.broadcast_to(t, (ds, LANES))


def _sample_index(page_table, iq_r, iw_r, ik_new, cache_ik, *, ds, nih, topk):
    db, n_pages = page_table.shape
    pps = min(PAGES_PER_STEP_INDEX, n_pages)
    assert n_pages % pps == 0
    n_steps = n_pages // pps
    page = cache_ik.shape[1]
    n_chunks = _round_up(n_pages + 1, SEARCH_UNROLL_SAMPLE)
    rows_i = nih * ds
    idx_bits = max(1, int(np.ceil(np.log2(n_chunks * LANES + 1))))
    per_b = lambda shape: pl.BlockSpec((1,) + shape, lambda b, s, pt: (b,) + (0,) * len(shape))

    grid_spec = pltpu.PrefetchScalarGridSpec(
        num_scalar_prefetch=1,
        grid=(db, n_steps),
        in_specs=[per_b((rows_i, LANES)), per_b((rows_i, LANES)), per_b((LANES, LANES)),
                  pl.BlockSpec(memory_space=pl.ANY)],
        out_specs=(per_b((n_chunks, ds, LANES)), per_b((ds, LANES))),
        scratch_shapes=[pltpu.VMEM((2, pps, page, LANES), _f32), pltpu.SemaphoreType.DMA((2,))],
    )
    return pl.pallas_call(
        functools.partial(_sample_index_kernel, pps=pps, n_steps=n_steps, n_pages=n_pages,
                          n_chunks=n_chunks, ds=ds, nih=nih, topk=topk, idx_bits=idx_bits),
        grid_spec=grid_spec,
        out_shape=(jax.ShapeDtypeStruct((db, n_chunks, ds, LANES), jnp.int32),
                   jax.ShapeDtypeStruct((db, ds, LANES), jnp.int32)),
        compiler_params=_cparams(("arbitrary", "arbitrary")),
        name="sample_index",
    )(page_table.reshape(-1), iq_r, iw_r, ik_new, cache_ik)


def _sample_attend_kernel(pt_ref, q_ref, key_ref, thr_ref, kn_ref, vn_ref, ck_ref, cv_ref, o_ref,
                          kbuf, vbuf, sem, m_sc, l_sc, acc_sc,
                          *, pps, n_steps, n_pages, ds, n_heads, n_kv):
    s = pl.program_id(1)
    g = pl.program_id(0) * n_steps + s
    _ring_step(pt_ref, [(ck_ref, kbuf), (cv_ref, vbuf)], sem, pps, g, pl.num_programs(0) * n_steps)
    slot = lax.rem(g, 2)
    group = n_heads // n_kv
    rows_g = group * ds
    page = kbuf.shape[2] // n_kv

    @pl.when(s == 0)
    def _():
        m_sc[...] = jnp.full(m_sc.shape, NEG, _f32)
        l_sc[...] = jnp.zeros_like(l_sc)
        acc_sc[...] = jnp.zeros_like(acc_sc)

    def attend(chunk_ids, k_of, v_of):
        thr = thr_ref[0]
        bias = jnp.concatenate([jnp.where(key_ref[0, c] >= thr, 0.0, NEG) for c in chunk_ids], axis=1)
        bias = jnp.concatenate([bias] * group, axis=0)
        m_prev, l_prev, acc_prev = m_sc[...], l_sc[...], acc_sc[...]
        heads = [slice(n * rows_g, (n + 1) * rows_g) for n in range(n_kv)]
        lgs = [_dot_nt(q_ref[0, rs, :],
                       jnp.concatenate([k_of(j, n) for j in range(len(chunk_ids))], axis=0)) + bias
               for n, rs in enumerate(heads)]
        m_out = [jnp.maximum(m_prev[rs], jnp.max(lg, axis=1, keepdims=True)) for lg, rs in zip(lgs, heads)]
        ps = [jnp.exp2(lg - m_new[:, :1]) for lg, m_new in zip(lgs, m_out)]
        pvs = [_dot(p.astype(_bf16), jnp.concatenate([v_of(j, n) for j in range(len(chunk_ids))], axis=0))
               for n, p in enumerate(ps)]
        l_out, acc_out = [], []
        for rs, m_new, p, pv in zip(heads, m_out, ps, pvs):
            alpha = jnp.exp2(m_prev[rs] - m_new)
            l_out.append(alpha * l_prev[rs] + jnp.sum(p, axis=1, keepdims=True))
            acc_out.append(alpha * acc_prev[rs] + pv)
        m_sc[...] = jnp.concatenate(m_out, axis=0)
        l_sc[...] = jnp.concatenate(l_out, axis=0)
        acc_sc[...] = jnp.concatenate(acc_out, axis=0)

    def head_rows(buf, j, n):
        return buf[slot, j, pl.ds(n, page, stride=n_kv), :].astype(_bf16)

    attend([s * pps + j for j in range(pps)],
           lambda j, n: head_rows(kbuf, j, n), lambda j, n: head_rows(vbuf, j, n))

    @pl.when(s == n_steps - 1)
    def _():
        attend([n_pages],
               lambda j, n: kn_ref[0, :, n * LANES:(n + 1) * LANES],
               lambda j, n: vn_ref[0, :, n * LANES:(n + 1) * LANES])
        o_ref[0] = (acc_sc[...] / l_sc[...]).astype(_bf16)


def _sample_attend(page_table, q_r, keys, thr, k_new, v_new, cache_k, cache_v, *, ds, n_heads, n_kv):
    db, n_pages = page_table.shape
    pps = min(PAGES_PER_STEP_ATTEND, n_pages)
    assert n_pages % pps == 0
    n_steps = n_pages // pps
    n_chunks = keys.shape[1]
    rows_q = n_heads * ds
    kvw = n_kv * LANES
    page_rows = cache_k.shape[1]
    per_b = lambda shape: pl.BlockSpec((1,) + shape, lambda b, s, pt: (b,) + (0,) * len(shape))
    hbm = pl.BlockSpec(memory_space=pl.ANY)

    grid_spec = pltpu.PrefetchScalarGridSpec(
        num_scalar_prefetch=1,
        grid=(db, n_steps),
        in_specs=[per_b((rows_q, LANES)), per_b((n_chunks, ds, LANES)), per_b((ds, LANES)),
                  per_b((LANES, kvw)), per_b((LANES, kvw)), hbm, hbm],
        out_specs=per_b((rows_q, LANES)),
        scratch_shapes=[pltpu.VMEM((2, pps, page_rows, LANES), _f32),
                        pltpu.VMEM((2, pps, page_rows, LANES), _f32),
                        pltpu.SemaphoreType.DMA((2,)),
                        pltpu.VMEM((rows_q, LANES), _f32),
                        pltpu.VMEM((rows_q, LANES), _f32),
                        pltpu.VMEM((rows_q, LANES), _f32)],
    )
    return pl.pallas_call(
        functools.partial(_sample_attend_kernel, pps=pps, n_steps=n_steps, n_pages=n_pages, ds=ds,
                          n_heads=n_heads, n_kv=n_kv),
        grid_spec=grid_spec,
        out_shape=jax.ShapeDtypeStruct((db, rows_q, LANES), _bf16),
        compiler_params=_cparams(("arbitrary", "arbitrary")),
        name="sample_attend",
    )(page_table.reshape(-1), q_r, keys, thr, k_new, v_new, cache_k, cache_v)


def _rope_tables(pos, dim):
    half = dim // 2
    inv = ROPE_THETA ** (-jnp.arange(half, dtype=_f32) / half)
    ang = pos.astype(_f32)[:, None] * inv[None, :]
    cos, sin = jnp.cos(ang), jnp.sin(ang)
    return jnp.concatenate([cos, cos], axis=1), jnp.concatenate([-sin, sin], axis=1)


def kernel(x_prompt, x_sample, cache_k, cache_v, cache_ik, state_conv, page_table, meta,
           norm1_g, ffn1_w1, ffn1_w3, ffn1_w2, norm_mix_g, w_in, w_dw, b_dw, conv_ln_g, conv_ln_b,
           w_conv_out, b_conv_out, w_o, w_out, norm2_g, ffn2_w1, ffn2_w3, ffn2_w2, final_g):
    bsz, seq, d = x_prompt.shape
    db, ds, _ = x_sample.shape
    depth, n_pool, page, n_kv, hd = cache_k.shape
    idx_dim = cache_ik.shape[-1]
    n_meta = meta.shape[0]
    width = w_dw.shape[1]
    ch = w_dw.shape[2]
    attn_w = w_o.shape[1]
    n_heads = attn_w // hd
    kv_w = n_kv * hd
    in_w = w_in.shape[2]
    nih = (in_w - 2 * ch - attn_w - 2 * kv_w - idx_dim - 2 * d) // (idx_dim + 1)
    n_pages = page_table.shape[1]
    past = n_pages * page
    assert bsz == 1 and depth == 1 and hd == LANES and idx_dim == LANES and page == LANES
    assert 2 * ch + attn_w + 2 * kv_w + nih * idx_dim + idx_dim + nih + 2 * d == in_w
    assert (n_heads * ds) % 8 == 0 and ds % 8 == 0 and ds <= LANES

    t = n_meta + seq
    tp = _round_up(t, Q_TILE)
    ms = db * ds
    mp = _round_up(tp + ms, FFN_ROW_TILE)
    topk_p = min(MAX_TOPK, seq // 4)
    topk_s = min(MAX_TOPK, (past + ds) // 4)
    assert topk_p <= Q_TILE and topk_s <= LANES * (n_pages + 1)

    h0 = jnp.concatenate([meta, x_prompt[0], jnp.zeros((tp - t, d), _f32),
                          x_sample.reshape(ms, d), jnp.zeros((mp - tp - ms, d), _f32)], axis=0)
    pos = jnp.concatenate([jnp.arange(t, dtype=jnp.int32), jnp.zeros((tp - t,), jnp.int32),
                           jnp.tile(past + jnp.arange(ds, dtype=jnp.int32), db),
                           jnp.zeros((mp - tp - ms,), jnp.int32)])
    cos, sin = _rope_tables(pos, hd)

    vec = lambda a: a.reshape(1, -1).astype(_f32)
    cast = lambda a: a.astype(_bf16)
    offs = np.cumsum([0, ch, ch, attn_w, kv_w, kv_w, nih * idx_dim, idx_dim, nih, d, d])
    wi = w_in[0]
    cols = lambda a, b: cast(wi[:, offs[a]:offs[b]])

    h1, u = _ffn(h0, vec(norm1_g[0]), cast(ffn1_w1[0]), cast(ffn1_w3[0]), cast(ffn1_w2[0]),
                 vec(norm_mix_g[0]), emit_h=True)

    glu = _proj_glu(u, cols(0, 1), cols(1, 2))
    (q,) = _proj(u, cols(2, 3), mode="rope", cos=cos, sin=sin, scale=hd ** -0.5 * LOG2E,
                 out_dtypes=(_bf16,), name="proj_q")
    k32, k16 = _proj(u, cols(3, 4), mode="rope", cos=cos, sin=sin, out_dtypes=(_f32, _bf16), name="proj_k")
    v32, v16 = _proj(u, cols(4, 5), out_dtypes=(_f32, _bf16), name="proj_v")
    (iq,) = _proj(u, cols(5, 6), mode="rope", cos=cos, sin=sin, out_dtypes=(_bf16,), name="proj_iq")
    ik32, ik16 = _proj(u, cols(6, 7), mode="rope", cos=cos, sin=sin, out_dtypes=(_f32, _bf16), name="proj_ik")
    (iw,) = _proj(u, cols(7, 8), scale=(idx_dim ** -0.5) * (nih ** -0.5), name="proj_iw")
    (sg,) = _proj(u, cols(8, 10), mode="sigmoid", out_dtypes=(_bf16,), name="proj_gates")

    ca_p = _conv_prompt(glu, tp, w_dw[0], vec(b_dw[0]), vec(conv_ln_g[0]), vec(conv_ln_b[0]))
    glu_s = glu[tp:tp + ms].reshape(db, ds, ch)
    ext_rows = _round_up(width - 1 + ds, 8)
    ext_s = jnp.concatenate([state_conv[0], glu_s,
                             jnp.zeros((db, ext_rows - (width - 1 + ds), ch), _f32)], axis=1)
    ca_s = _conv_sample(ext_s, w_dw[0], vec(b_dw[0]), vec(conv_ln_g[0]), vec(conv_ln_b[0]), ds)
    ca = jnp.concatenate([ca_p, ca_s.reshape(ms, ch), jnp.zeros((mp - tp - ms, ch), _bf16)], axis=0)

    nc = tp // Q_TILE
    ikt = ik16[:tp].reshape(nc, Q_TILE, idx_dim).transpose(0, 2, 1)
    kt = k16[:tp].reshape(nc, Q_TILE, n_kv, hd).transpose(0, 2, 3, 1)
    vch = v16[:tp].reshape(nc, Q_TILE, kv_w)
    ao_p = _prompt_attention(iq, iw, q, ikt, kt, vch, tp=tp, nih=nih, n_heads=n_heads, n_kv=n_kv,
                             topk=topk_p)

    sl = slice(tp, tp + ms)
    iq_r = iq[sl].reshape(db, ds, nih, idx_dim).transpose(0, 2, 1, 3).reshape(db, nih * ds, idx_dim)
    iw_r = jnp.broadcast_to(iw[sl].reshape(db, ds, nih).transpose(0, 2, 1).reshape(db, nih * ds, 1),
                            (db, nih * ds, LANES))
    q_r = q[sl].reshape(db, ds, n_heads, hd).transpose(0, 2, 1, 3).reshape(db, n_heads * ds, hd)
    pad_keys = lambda a: jnp.pad(a.reshape(db, ds, -1), ((0, 0), (0, LANES - ds), (0, 0)))
    keys_s, thr_s = _sample_index(page_table, iq_r, iw_r, pad_keys(ik16[sl]), cache_ik[0],
                                  ds=ds, nih=nih, topk=topk_s)
    ao_s = _sample_attend(page_table, q_r, keys_s, thr_s, pad_keys(k16[sl]), pad_keys(v16[sl]),
                          cache_k[0].reshape(n_pool, page * n_kv, hd),
                          cache_v[0].reshape(n_pool, page * n_kv, hd),
                          ds=ds, n_heads=n_heads, n_kv=n_kv)
    ao_s = ao_s.reshape(db, n_heads, ds, hd).transpose(0, 2, 1, 3).reshape(ms, attn_w)
    ao = jnp.concatenate([ao_p, ao_s, jnp.zeros((mp - tp - ms, attn_w), _bf16)], axis=0)

    m = _merge(ca, ao, cast(w_conv_out[0]), cast(w_o[0]), vec(b_conv_out[0]), sg)
    h2 = _resid_proj(m, cast(w_out[0]), h1)
    y = _ffn(h2, vec(norm2_g[0]), cast(ffn2_w1[0]), cast(ffn2_w3[0]), cast(ffn2_w2[0]),
             vec(final_g), emit_h=False)

    y_prompt = y[n_meta:t][None]
    y_sample = y[sl].reshape(db, ds, d)
    hist = width - 1
    new_conv_p = glu[t - hist:t][None, None]
    new_conv_s = jnp.concatenate([state_conv[0], glu_s], axis=1)[:, -hist:][None]
    return (y_prompt, y_sample,
            k32[:t].reshape(1, 1, t, n_kv, hd), v32[:t].reshape(1, 1, t, n_kv, hd),
            ik32[:t].reshape(1, 1, t, idx_dim), new_conv_p,
            k32[sl].reshape(1, db, ds, n_kv, hd), v32[sl].reshape(1, db, ds, n_kv, hd),
            ik32[sl].reshape(1, db, ds, idx_dim), new_conv_s)
```

```python
import functools

import numpy as np
import jax
import jax.numpy as jnp
from jax import lax
from jax.experimental import pallas as pl
from jax.experimental.pallas import tpu as pltpu

MAX_TOPK = 256
ROPE_THETA = 10000.0
EPS = 1e-6

LANES = 128
FFN_ROW_TILE = 512
FF_TILE = 512
PROJ_ROW_CAP = 1088
PROJ_COL_TILE = 1024
Q_TILE = 256
PAGES_PER_STEP_INDEX = 32
PAGES_PER_STEP_ATTEND = 8
SEARCH_UNROLL_SAMPLE = 16
SEARCH_ROWS = 128
VMEM_LIMIT = 56 * 1024 * 1024
INT_MIN = -(2 ** 31)
NEG = -1e30
LOG2E = 1.4426950408889634

_bf16 = jnp.bfloat16
_f32 = jnp.float32


def _cparams(sem):
    return pltpu.CompilerParams(dimension_semantics=sem, vmem_limit_bytes=VMEM_LIMIT)


def _dot(a, b):
    return jnp.dot(a, b, preferred_element_type=_f32)


def _dot_nt(a, b):
    return lax.dot_general(a, b, (((1,), (1,)), ((), ())), preferred_element_type=_f32)


def _rms(x, g):
    return x * lax.rsqrt(jnp.mean(x * x, axis=-1, keepdims=True) + EPS) * g


def _sigmoid(x):
    return 1.0 / (1.0 + jnp.exp(-x))


def _round_up(x, m):
    return (x + m - 1) // m * m


def _row_tile(rows, cap):
    return max(t for t in range(16, cap + 1, 16) if rows % t == 0)


def _col_tile(n):
    return PROJ_COL_TILE if n % PROJ_COL_TILE == 0 else (512 if n % 512 == 0 else n)


def _ffn_kernel(h_ref, g_ref, w1_ref, w3_ref, w2_ref, g2_ref, *refs, n_f, emit_h):
    if emit_h:
        h_out_ref, u_out_ref, u_sc, acc_sc = refs
    else:
        y_out_ref, u_sc, acc_sc = refs
    f = pl.program_id(1)

    @pl.when(f == 0)
    def _():
        u_sc[...] = _rms(h_ref[...], g_ref[...]).astype(_bf16)
        acc_sc[...] = jnp.zeros_like(acc_sc)

    u = u_sc[...]
    a = _dot(u, w1_ref[...])
    b = _dot(u, w3_ref[...])
    gate = (a * _sigmoid(a)) * b
    acc_sc[...] += _dot(gate.astype(_bf16), w2_ref[...])

    @pl.when(f == n_f - 1)
    def _():
        hn = h_ref[...] + 0.5 * acc_sc[...]
        un = _rms(hn, g2_ref[...])
        if emit_h:
            h_out_ref[...] = hn
            u_out_ref[...] = un.astype(_bf16)
        else:
            y_out_ref[...] = un


def _ffn(h, g, w1, w3, w2, g2, *, emit_h):
    mp, d = h.shape
    dff = w1.shape[1]
    tm = FFN_ROW_TILE
    tf = min(FF_TILE, dff)
    assert mp % tm == 0 and dff % tf == 0
    n_f = dff // tf
    row = pl.BlockSpec((tm, d), lambda i, f: (i, 0))
    vec = pl.BlockSpec((1, d), lambda i, f: (0, 0))
    if emit_h:
        out_shape = (jax.ShapeDtypeStruct((mp, d), _f32), jax.ShapeDtypeStruct((mp, d), _bf16))
        out_specs = (row, row)
    else:
        out_shape = jax.ShapeDtypeStruct((mp, d), _f32)
        out_specs = row
    return pl.pallas_call(
        functools.partial(_ffn_kernel, n_f=n_f, emit_h=emit_h),
        grid=(mp // tm, n_f),
        in_specs=[row, vec,
                  pl.BlockSpec((d, tf), lambda i, f: (0, f)),
                  pl.BlockSpec((d, tf), lambda i, f: (0, f)),
                  pl.BlockSpec((tf, d), lambda i, f: (f, 0)),
                  vec],
        out_specs=out_specs,
        out_shape=out_shape,
        scratch_shapes=[pltpu.VMEM((tm, d), _bf16), pltpu.VMEM((tm, d), _f32)],
        compiler_params=_cparams(("parallel", "arbitrary")),
        name="ffn_emit_h" if emit_h else "ffn_final",
    )(h, g, w1, w3, w2, g2)


def _glu_kernel(u_ref, wa_ref, wb_ref, o_ref):
    u = u_ref[...]
    o_ref[...] = _dot(u, wa_ref[...]) * _sigmoid(_dot(u, wb_ref[...]))


def _proj_glu(u, wa, wb):
    mp, d = u.shape
    n = wa.shape[1]
    tm, tn = _row_tile(mp, PROJ_ROW_CAP), _col_tile(n)
    return pl.pallas_call(
        _glu_kernel,
        grid=(mp // tm, n // tn),
        in_specs=[pl.BlockSpec((tm, d), lambda i, j: (i, 0)),
                  pl.BlockSpec((d, tn), lambda i, j: (0, j)),
                  pl.BlockSpec((d, tn), lambda i, j: (0, j))],
        out_specs=pl.BlockSpec((tm, tn), lambda i, j: (i, j)),
        out_shape=jax.ShapeDtypeStruct((mp, n), _f32),
        compiler_params=_cparams(("parallel", "parallel")),
        name="proj_glu",
    )(u, wa, wb)


def _proj_kernel(u_ref, w_ref, *refs, mode, scale, out_dtypes):
    if mode == "rope":
        cos_ref, sin_ref = refs[:2]
        outs = refs[2:]
    else:
        outs = refs
    z = _dot(u_ref[...], w_ref[...])
    if mode == "rope":
        cos = cos_ref[...]
        sin = sin_ref[...]
        for j in range(z.shape[1] // LANES):
            cs = slice(j * LANES, (j + 1) * LANES)
            x = z[:, cs]
            r = x * cos + pltpu.roll(x, LANES // 2, 1) * sin
            if scale != 1.0:
                r = r * scale
            for o_ref, dt in zip(outs, out_dtypes):
                o_ref[:, cs] = r.astype(dt)
        return
    if mode == "sigmoid":
        z = _sigmoid(z)
    if scale != 1.0:
        z = z * scale
    for o_ref, dt in zip(outs, out_dtypes):
        o_ref[...] = z.astype(dt)


def _proj(u, w, *, mode="plain", cos=None, sin=None, scale=1.0, out_dtypes=(_f32,), name="proj"):
    mp, d = u.shape
    n = w.shape[1]
    tm, tn = _row_tile(mp, PROJ_ROW_CAP), _col_tile(n)
    in_specs = [pl.BlockSpec((tm, d), lambda i, j: (i, 0)),
                pl.BlockSpec((d, tn), lambda i, j: (0, j))]
    args = [u, w]
    if mode == "rope":
        in_specs += [pl.BlockSpec((tm, LANES), lambda i, j: (i, 0))] * 2
        args += [cos, sin]
    ospec = pl.BlockSpec((tm, tn), lambda i, j: (i, j))
    outs = pl.pallas_call(
        functools.partial(_proj_kernel, mode=mode, scale=scale, out_dtypes=out_dtypes),
        grid=(mp // tm, n // tn),
        in_specs=in_specs,
        out_specs=tuple(ospec for _ in out_dtypes),
        out_shape=tuple(jax.ShapeDtypeStruct((mp, n), dt) for dt in out_dtypes),
        compiler_params=_cparams(("parallel", "parallel")),
        name=name,
    )(*args)
    return outs


def _ln_swish(c, g, b):
    mu = jnp.mean(c, axis=-1, keepdims=True)
    var = jnp.mean(jnp.square(c - mu), axis=-1, keepdims=True)
    y = (c - mu) * lax.rsqrt(var + EPS) * g + b
    return y * _sigmoid(y)


def _conv_prompt_kernel(cur_ref, halo_ref, w_ref, bdw_ref, g_ref, b_ref, o_ref,
                        ext_sc, sh_sc, wb_sc, c_sc, *, width, tc, halo, cblk):
    i = pl.program_id(0)
    ch = cur_ref.shape[1]
    rb = 16
    ext_sc[pl.ds(0, halo), :] = jnp.where(i > 0, halo_ref[...], 0.0)
    ext_sc[pl.ds(halo, tc), :] = cur_ref[...]
    ext_sc[pl.ds(halo + tc, 8), :] = jnp.zeros((8, ch), _f32)
    lead = halo - (width - 1)
    for cb in range(ch // cblk):
        cs = pl.ds(cb * cblk, cblk)
        for r in range(8):
            sh_sc[r] = ext_sc[pl.ds(r, halo + tc), cs]
        for j in range(width):
            wb_sc[j] = jnp.broadcast_to(w_ref[pl.ds(j, 1), cs], (8, cblk))
        bias = jnp.broadcast_to(bdw_ref[:, cs], (8, cblk))

        def rows(t, _):
            t0 = pl.multiple_of(t * rb, rb)
            accs = [bias] * (rb // 8)
            for j in range(width):
                off = lead + j
                wv = wb_sc[j]
                for a in range(rb // 8):
                    x = sh_sc[off % 8, pl.ds(t0 + a * 8 + (off // 8) * 8, 8), :]
                    accs[a] = accs[a] + x * wv
            for a in range(rb // 8):
                c_sc[pl.ds(t0 + a * 8, 8), cs] = accs[a]
            return 0

        lax.fori_loop(0, tc // rb, rows, 0)
    o_ref[...] = _ln_swish(c_sc[...], g_ref[...], b_ref[...]).astype(_bf16)


def _conv_prompt(glu, tp, w_dw, b_dw, ln_g, ln_b):
    ch = glu.shape[1]
    width = w_dw.shape[0]
    tc = Q_TILE
    halo = 32
    assert width - 1 <= halo and tp % tc == 0 and tc % halo == 0
    cblk = min(512, ch)
    vec = pl.BlockSpec((1, ch), lambda i: (0, 0))
    return pl.pallas_call(
        functools.partial(_conv_prompt_kernel, width=width, tc=tc, halo=halo, cblk=cblk),
        grid=(tp // tc,),
        in_specs=[pl.BlockSpec((tc, ch), lambda i: (i, 0)),
                  pl.BlockSpec((halo, ch), lambda i: (jnp.maximum(i * (tc // halo) - 1, 0), 0)),
                  pl.BlockSpec((width, ch), lambda i: (0, 0)),
                  vec, vec, vec],
        out_specs=pl.BlockSpec((tc, ch), lambda i: (i, 0)),
        out_shape=jax.ShapeDtypeStruct((tp, ch), _bf16),
        scratch_shapes=[pltpu.VMEM((halo + tc + 8, ch), _f32),
                        pltpu.VMEM((8, halo + tc, cblk), _f32),
                        pltpu.VMEM((width, 8, cblk), _f32),
                        pltpu.VMEM((tc, ch), _f32)],
        compiler_params=_cparams(("parallel",)),
        name="conv_prompt",
    )(glu, glu, w_dw, b_dw, ln_g, ln_b)


def _conv_sample_kernel(ext_ref, w_ref, bdw_ref, g_ref, b_ref, o_ref, *, width, ds):
    ch = ext_ref.shape[2]
    acc = jnp.broadcast_to(bdw_ref[...], (ds, ch))
    for j in range(width):
        acc = acc + ext_ref[0, pl.ds(j, ds), :] * w_ref[pl.ds(j, 1), :]
    o_ref[0] = _ln_swish(acc, g_ref[...], b_ref[...]).astype(_bf16)


def _conv_sample(ext, w_dw, b_dw, ln_g, ln_b, ds):
    db, rows, ch = ext.shape
    width = w_dw.shape[0]
    vec = pl.BlockSpec((1, ch), lambda b: (0, 0))
    return pl.pallas_call(
        functools.partial(_conv_sample_kernel, width=width, ds=ds),
        grid=(db,),
        in_specs=[pl.BlockSpec((1, rows, ch), lambda b: (b, 0, 0)),
                  pl.BlockSpec((width, ch), lambda b: (0, 0)),
                  vec, vec, vec],
        out_specs=pl.BlockSpec((1, ds, ch), lambda b: (b, 0, 0)),
        out_shape=jax.ShapeDtypeStruct((db, ds, ch), _bf16),
        compiler_params=_cparams(("parallel",)),
        name="conv_sample",
    )(ext, w_dw, b_dw, ln_g, ln_b)


def _merge_kernel(ca_ref, ao_ref, wco_ref, wo_ref, bco_ref, sgc_ref, sga_ref, m_ref):
    yc = _dot(ca_ref[...], wco_ref[...]) + bco_ref[...]
    ya = _dot(ao_ref[...], wo_ref[...])
    m_ref[...] = (sgc_ref[...].astype(_f32) * yc + sga_ref[...].astype(_f32) * ya).astype(_bf16)


def _merge(ca, ao, wco, wo, bco, sg):
    mp, ch = ca.shape
    aw = ao.shape[1]
    d = wco.shape[1]
    tm = _row_tile(mp, PROJ_ROW_CAP)
    tn = min(512, d)
    nj = d // tn
    return pl.pallas_call(
        _merge_kernel,
        grid=(mp // tm, nj),
        in_specs=[pl.BlockSpec((tm, ch), lambda i, j: (i, 0)),
                  pl.BlockSpec((tm, aw), lambda i, j: (i, 0)),
                  pl.BlockSpec((ch, tn), lambda i, j: (0, j)),
                  pl.BlockSpec((aw, tn), lambda i, j: (0, j)),
                  pl.BlockSpec((1, tn), lambda i, j: (0, j)),
                  pl.BlockSpec((tm, tn), lambda i, j: (i, j)),
                  pl.BlockSpec((tm, tn), lambda i, j: (i, j + nj))],
        out_specs=pl.BlockSpec((tm, tn), lambda i, j: (i, j)),
        out_shape=jax.ShapeDtypeStruct((mp, d), _bf16),
        compiler_params=_cparams(("parallel", "parallel")),
        name="merge",
    )(ca, ao, wco, wo, bco, sg, sg)


def _resid_kernel(m_ref, w_ref, h_ref, o_ref):
    o_ref[...] = h_ref[...] + _dot(m_ref[...], w_ref[...])


def _resid_proj(m, w, h):
    mp, d = m.shape
    n = w.shape[1]
    tm, tn = _row_tile(mp, PROJ_ROW_CAP), _col_tile(n)
    return pl.pallas_call(
        _resid_kernel,
        grid=(mp // tm, n // tn),
        in_specs=[pl.BlockSpec((tm, d), lambda i, j: (i, 0)),
                  pl.BlockSpec((d, tn), lambda i, j: (0, j)),
                  pl.BlockSpec((tm, tn), lambda i, j: (i, j))],
        out_specs=pl.BlockSpec((tm, tn), lambda i, j: (i, j)),
        out_shape=jax.ShapeDtypeStruct((mp, n), _f32),
        compiler_params=_cparams(("parallel", "parallel")),
        name="resid_proj",
    )(m, w, h)


def _order_key(s):
    b = lax.bitcast_convert_type(s, jnp.int32)
    return b ^ (lax.shift_right_arithmetic(b, 31) & jnp.int32(0x7FFFFFFF))


def _bit_search(count_ge, rows, n_bits, k, lowest):
    lowest = jnp.int32(lowest)

    def body(b, t_u):
        bit = lax.shift_left(jnp.int32(1), jnp.int32(n_bits - 1) - jnp.asarray(b, jnp.int32))
        cand = t_u | bit
        cnt = count_ge(cand + lowest)
        return jnp.where(cnt >= k, cand, t_u)

    t_u = lax.fori_loop(0, n_bits, body, jnp.zeros((rows, 1), jnp.int32))
    return t_u + lowest


def _kth_key_packed(load, pk_ref, scan, rows, width, k):
    halves = width // LANES
    i16 = jnp.int16
    half_min, half_max = -(2 ** 15), 2 ** 15 - 1

    def count16(th):
        th16 = th.astype(i16)

        def one(c, acc):
            w = jnp.where(pk_ref[c] >= th16, i16(1), i16(0))
            for j in range(halves):
                acc = acc + w[:, j * LANES:(j + 1) * LANES]
            return acc
        acc = scan(one, jnp.zeros((rows, LANES), i16))
        return jnp.sum(acc.astype(_f32), axis=1, keepdims=True)

    def fill_top(c, carry):
        pk_ref[c] = lax.shift_right_arithmetic(load(c), 16).astype(i16)
        return carry
    scan(fill_top, 0)
    top = _bit_search(count16, rows, 16, k, half_min)
    above = jnp.where(top >= half_max, 0.0, count16(jnp.minimum(top + 1, half_max)))

    def fill_low(c, carry):
        x = load(c)
        low = (x & jnp.int32(0xFFFF)) + half_min
        pk_ref[c] = jnp.where(lax.shift_right_arithmetic(x, 16) == top, low, half_min).astype(i16)
        return carry
    scan(fill_low, 0)
    low = _bit_search(count16, rows, 16, k - above, half_min)
    return lax.shift_left(top, 16) | (low - half_min)


def _select_threshold(load, store, n_iters, unroll, rows, width, k, idx_bits, pk_ref=None):
    halves = width // LANES

    def scan(fn, init):
        def body(it, carry):
            for u in range(unroll):
                carry = fn(it * unroll + u, carry)
            return carry
        return lax.fori_loop(0, n_iters, body, init)

    def count(pred):
        def one(c, acc):
            m = pred(load(c), c)
            for j in range(halves):
                acc = acc + jnp.where(m[:, j * LANES:(j + 1) * LANES], 1.0, 0.0)
            return acc
        return jnp.sum(scan(one, jnp.zeros((rows, LANES), _f32)), axis=1, keepdims=True)

    kf = jnp.float32(k)
    if pk_ref is None:
        t = _bit_search(lambda th: count(lambda x, c: x >= th), rows, 32, kf, INT_MIN)
    else:
        t = _kth_key_packed(load, pk_ref, scan, rows, width, kf)
    t = jnp.maximum(t, jnp.int32(INT_MIN + 1))
    n_ge = count(lambda x, c: x >= t)

    @pl.when(jnp.max(n_ge) > kf)
    def _():
        n_gt = count(lambda x, c: x > t)
        need = kf - n_gt
        lane = lax.broadcasted_iota(jnp.int32, (rows, width), 1)

        def ties_below(j):
            return count(lambda x, c: (x == t) & ((lane + c * width) < j))

        j_cut = _bit_search(lambda j: need - ties_below(j) + (kf - 1.0), rows, idx_bits, kf, 0)
        surplus = n_ge > kf

        def rewrite(c, carry):
            x = load(c)
            drop = (x == t) & ((lane + c * width) > j_cut) & surplus
            store(c, jnp.where(drop, jnp.int32(INT_MIN), x))
            return carry
        scan(rewrite, 0)

    return t


def _prompt_attn_kernel(iq_ref, iw_ref, q_ref, ikt_ref, kt_ref, v_ref, o_ref,
                        key_sc, pk_sc, thr_sc, wb_sc, m_sc, acc_sc,
                        *, nih, n_heads, n_kv, topk, idx_bits, sub):
    i = pl.program_id(0)
    tq = q_ref.shape[0]
    group = n_heads // n_kv
    unroll = 2

    for h in range(nih):
        wb_sc[h] = jnp.broadcast_to(iw_ref[:, h:h + 1], (tq, LANES))

    def scores(c):
        rhs = ikt_ref[c]
        halves = [jnp.zeros((tq, LANES), _f32) for _ in range(tq // LANES)]
        for h in range(nih):
            d = _dot(iq_ref[:, h * LANES:(h + 1) * LANES], rhs)
            w = wb_sc[h]
            for j in range(len(halves)):
                halves[j] = halves[j] + jnp.maximum(d[:, j * LANES:(j + 1) * LANES], 0.0) * w
        return _order_key(jnp.concatenate(halves, axis=1))

    def score_body(c, _):
        key_sc[c] = scores(c)
        return 0

    lax.fori_loop(0, i, score_body, 0)
    row = lax.broadcasted_iota(jnp.int32, (tq, tq), 0)
    col = lax.broadcasted_iota(jnp.int32, (tq, tq), 1)
    key_sc[i] = jnp.where(col <= row, scores(i), jnp.int32(INT_MIN))
    key_sc[i + 1] = jnp.full((tq, tq), INT_MIN, jnp.int32)
    n_iters = lax.shift_right_logical(i + unroll, 1)

    for rb in range(tq // sub):
        rs = pl.ds(rb * sub, sub)
        t = _select_threshold(lambda c: key_sc[c, rs, :],
                              lambda c, x: key_sc.__setitem__((c, rs, slice(None)), x),
                              n_iters, unroll, sub, tq, topk, idx_bits, pk_ref=pk_sc)
        thr_sc[rs, :] = jnp.broadcast_to(t, (sub, LANES))

    thr = jnp.concatenate([thr_sc[...]] * (tq // LANES), axis=1)
    m_sc[...] = jnp.full(m_sc.shape, NEG, _f32)
    acc_sc[...] = jnp.zeros_like(acc_sc)
    ones = jnp.ones((tq, LANES), _bf16)

    def attend(c, _):
        bias = jnp.where(key_sc[c] >= thr, 0.0, NEG)
        vc = v_ref[c]

        for n in range(n_kv):
            kt = kt_ref[c, n]
            vn1 = jnp.concatenate([vc[:, n * LANES:(n + 1) * LANES], ones], axis=1)
            for g in range(group):
                h = n * group + g
                lg = _dot(q_ref[:, h * LANES:(h + 1) * LANES], kt) + bias
                m_prev = m_sc[h]
                m_new = jnp.maximum(m_prev, jnp.max(lg, axis=1, keepdims=True))
                alpha = jnp.exp2(m_prev - m_new)
                p = jnp.exp2(lg - jnp.concatenate([m_new] * (tq // LANES), axis=1))
                acc_sc[h] = jnp.concatenate([alpha, alpha], axis=1) * acc_sc[h] + _dot(p.astype(_bf16), vn1)
                m_sc[h] = m_new
        return 0

    lax.fori_loop(0, i + 1, attend, 0)
    for h in range(n_heads):
        a = acc_sc[h]
        o_ref[:, h * LANES:(h + 1) * LANES] = (a[:, :LANES] / a[:, LANES:]).astype(_bf16)


def _prompt_attention(iq, iw, q, ikt, kt, v, *, tp, nih, n_heads, n_kv, topk):
    tq = Q_TILE
    nc = tp // tq
    resident = lambda shape: pl.BlockSpec(shape, lambda i: (0,) * len(shape),
                                          pipeline_mode=pl.Buffered(1))
    idx_bits = max(1, int(np.ceil(np.log2(tp + 1))))
    sub = min(SEARCH_ROWS, tq)
    return pl.pallas_call(
        functools.partial(_prompt_attn_kernel, nih=nih, n_heads=n_heads, n_kv=n_kv, topk=topk,
                          idx_bits=idx_bits, sub=sub),
        grid=(nc,),
        in_specs=[pl.BlockSpec((tq, nih * LANES), lambda i: (i, 0)),
                  pl.BlockSpec((tq, nih), lambda i: (i, 0)),
                  pl.BlockSpec((tq, n_heads * LANES), lambda i: (i, 0)),
                  resident((nc, LANES, tq)),
                  resident((nc, n_kv, LANES, tq)),
                  resident((nc, tq, n_kv * LANES))],
        out_specs=pl.BlockSpec((tq, n_heads * LANES), lambda i: (i, 0)),
        out_shape=jax.ShapeDtypeStruct((tp, n_heads * LANES), _bf16),
        scratch_shapes=[pltpu.VMEM((nc + 1, tq, tq), jnp.int32),
                        pltpu.VMEM((nc + 1, sub, tq), jnp.int16),
                        pltpu.VMEM((tq, LANES), jnp.int32),
                        pltpu.VMEM((nih, tq, LANES), _f32),
                        pltpu.VMEM((n_heads, tq, LANES), _f32),
                        pltpu.VMEM((n_heads, tq, 2 * LANES), _f32)],
        compiler_params=_cparams(("arbitrary",)),
        name="prompt_attention",
    )(iq, iw, q, ikt, kt, v)


def _ring_copies(pt_ref, streams, sem_ref, pps, g):
    slot = lax.rem(g, 2)
    return [pltpu.make_async_copy(cache.at[pt_ref[g * pps + j]], buf.at[slot, j], sem_ref.at[slot])
            for j in range(pps) for cache, buf in streams]


def _ring_step(pt_ref, streams, sem_ref, pps, g, n_total):
    def request(step):
        for n, cp in enumerate(_ring_copies(pt_ref, streams, sem_ref, pps, step)):
            cp.start(priority=n % 2)

    @pl.when(g == 0)
    def _():
        request(g)

    @pl.when(g + 1 < n_total)
    def _():
        request(g + 1)

    for cp in _ring_copies(pt_ref, streams, sem_ref, pps, g):
        cp.wait()


def _sample_index_kernel(pt_ref, iq_ref, iw_ref, ikn_ref, cache_ref, key_ref, thr_ref, buf, sem,
                         *, pps, n_steps, n_pages, n_chunks, ds, nih, topk, idx_bits):
    s = pl.program_id(1)
    g = pl.program_id(0) * n_steps + s
    _ring_step(pt_ref, [(cache_ref, buf)], sem, pps, g, pl.num_programs(0) * n_steps)
    slot = lax.rem(g, 2)

    def index_keys(ik_page):
        x = jnp.maximum(_dot_nt(iq_ref[0], ik_page), 0.0) * iw_ref[0]
        return _order_key(jnp.sum(x.reshape(nih, ds, LANES), axis=0))

    for j in range(pps):
        key_ref[0, s * pps + j] = index_keys(buf[slot, j].astype(_bf16))

    @pl.when(s == n_steps - 1)
    def _():
        qi = lax.broadcasted_iota(jnp.int32, (ds, LANES), 0)
        ki = lax.broadcasted_iota(jnp.int32, (ds, LANES), 1)
        key_ref[0, n_pages] = jnp.where(ki <= qi, index_keys(ikn_ref[0]), jnp.int32(INT_MIN))
        for c in range(n_pages + 1, n_chunks):
            key_ref[0, c] = jnp.full((ds, LANES), INT_MIN, jnp.int32)
        t = _select_threshold(lambda c: key_ref[0, c],
                              lambda c, x: key_ref.__setitem__((0, c), x),
                              n_chunks // SEARCH_UNROLL_SAMPLE, SEARCH_UNROLL_SAMPLE,
                              ds, LANES, topk, idx_bits)
        thr_ref[0] = jnp.broadcast_to(t, (ds, LANES))


def _sample_index(page_table, iq_r, iw_r, ik_new, cache_ik, *, ds, nih, topk):
    db, n_pages = page_table.shape
    pps = min(PAGES_PER_STEP_INDEX, n_pages)
    assert n_pages % pps == 0
    n_steps = n_pages // pps
    page = cache_ik.shape[1]
    n_chunks = _round_up(n_pages + 1, SEARCH_UNROLL_SAMPLE)
    rows_i = nih * ds
    idx_bits = max(1, int(np.ceil(np.log2(n_chunks * LANES + 1))))
    per_b = lambda shape: pl.BlockSpec((1,) + shape, lambda b, s, pt: (b,) + (0,) * len(shape))

    grid_spec = pltpu.PrefetchScalarGridSpec(
        num_scalar_prefetch=1,
        grid=(db, n_steps),
        in_specs=[per_b((rows_i, LANES)), per_b((rows_i, LANES)), per_b((LANES, LANES)),
                  pl.BlockSpec(memory_space=pl.ANY)],
        out_specs=(per_b((n_chunks, ds, LANES)), per_b((ds, LANES))),
        scratch_shapes=[pltpu.VMEM((2, pps, page, LANES), _f32), pltpu.SemaphoreType.DMA((2,))],
    )
    return pl.pallas_call(
        functools.partial(_sample_index_kernel, pps=pps, n_steps=n_steps, n_pages=n_pages,
                          n_chunks=n_chunks, ds=ds, nih=nih, topk=topk, idx_bits=idx_bits),
        grid_spec=grid_spec,
        out_shape=(jax.ShapeDtypeStruct((db, n_chunks, ds, LANES), jnp.int32),
                   jax.ShapeDtypeStruct((db, ds, LANES), jnp.int32)),
        compiler_params=_cparams(("arbitrary", "arbitrary")),
        name="sample_index",
    )(page_table.reshape(-1), iq_r, iw_r, ik_new, cache_ik)


def _sample_attend_kernel(pt_ref, q_ref, key_ref, thr_ref, kn_ref, vn_ref, ck_ref, cv_ref, o_ref,
                          kbuf, vbuf, sem, m_sc, l_sc, acc_sc,
                          *, pps, n_steps, n_pages, ds, n_heads, n_kv):
    s = pl.program_id(1)
    g = pl.program_id(0) * n_steps + s
    _ring_step(pt_ref, [(ck_ref, kbuf), (cv_ref, vbuf)], sem, pps, g, pl.num_programs(0) * n_steps)
    slot = lax.rem(g, 2)
    group = n_heads // n_kv
    rows_g = group * ds
    page = kbuf.shape[2] // n_kv

    @pl.when(s == 0)
    def _():
        m_sc[...] = jnp.full(m_sc.shape, NEG, _f32)
        l_sc[...] = jnp.zeros_like(l_sc)
        acc_sc[...] = jnp.zeros_like(acc_sc)

    def attend(chunk_ids, k_of, v_of):
        thr = thr_ref[0]
        bias = jnp.concatenate([jnp.where(key_ref[0, c] >= thr, 0.0, NEG) for c in chunk_ids], axis=1)
        bias = jnp.concatenate([bias] * group, axis=0)
        m_prev, l_prev, acc_prev = m_sc[...], l_sc[...], acc_sc[...]
        heads = [slice(n * rows_g, (n + 1) * rows_g) for n in range(n_kv)]
        lgs = [_dot_nt(q_ref[0, rs, :],
                       jnp.concatenate([k_of(j, n) for j in range(len(chunk_ids))], axis=0)) + bias
               for n, rs in enumerate(heads)]
        m_out = [jnp.maximum(m_prev[rs], jnp.max(lg, axis=1, keepdims=True)) for lg, rs in zip(lgs, heads)]
        ps = [jnp.exp2(lg - m_new[:, :1]) for lg, m_new in zip(lgs, m_out)]
        pvs = [_dot(p.astype(_bf16), jnp.concatenate([v_of(j, n) for j in range(len(chunk_ids))], axis=0))
               for n, p in enumerate(ps)]
        l_out, acc_out = [], []
        for rs, m_new, p, pv in zip(heads, m_out, ps, pvs):
            alpha = jnp.exp2(m_prev[rs] - m_new)
            l_out.append(alpha * l_prev[rs] + jnp.sum(p, axis=1, keepdims=True))
            acc_out.append(alpha * acc_prev[rs] + pv)
        m_sc[...] = jnp.concatenate(m_out, axis=0)
        l_sc[...] = jnp.concatenate(l_out, axis=0)
        acc_sc[...] = jnp.concatenate(acc_out, axis=0)

    def head_rows(buf, j, n):
        return buf[slot, j, pl.ds(n, page, stride=n_kv), :].astype(_bf16)

    attend([s * pps + j for j in range(pps)],
           lambda j, n: head_rows(kbuf, j, n), lambda j, n: head_rows(vbuf, j, n))

    @pl.when(s == n_steps - 1)
    def _():
        attend([n_pages],
               lambda j, n: kn_ref[0, :, n * LANES:(n + 1) * LANES],
               lambda j, n: vn_ref[0, :, n * LANES:(n + 1) * LANES])
        o_ref[0] = (acc_sc[...] / l_sc[...]).astype(_bf16)


def _sample_attend(page_table, q_r, keys, thr, k_new, v_new, cache_k, cache_v, *, ds, n_heads, n_kv):
    db, n_pages = page_table.shape
    pps = min(PAGES_PER_STEP_ATTEND, n_pages)
    assert n_pages % pps == 0
    n_steps = n_pages // pps
    n_chunks = keys.shape[1]
    rows_q = n_heads * ds
    kvw = n_kv * LANES
    page_rows = cache_k.shape[1]
    per_b = lambda shape: pl.BlockSpec((1,) + shape, lambda b, s, pt: (b,) + (0,) * len(shape))
    hbm = pl.BlockSpec(memory_space=pl.ANY)

    grid_spec = pltpu.PrefetchScalarGridSpec(
        num_scalar_prefetch=1,
        grid=(db, n_steps),
        in_specs=[per_b((rows_q, LANES)), per_b((n_chunks, ds, LANES)), per_b((ds, LANES)),
                  per_b((LANES, kvw)), per_b((LANES, kvw)), hbm, hbm],
        out_specs=per_b((rows_q, LANES)),
        scratch_shapes=[pltpu.VMEM((2, pps, page_rows, LANES), _f32),
                        pltpu.VMEM((2, pps, page_rows, LANES), _f32),
                        pltpu.SemaphoreType.DMA((2,)),
                        pltpu.VMEM((rows_q, LANES), _f32),
                        pltpu.VMEM((rows_q, LANES), _f32),
                        pltpu.VMEM((rows_q, LANES), _f32)],
    )
    return pl.pallas_call(
        functools.partial(_sample_attend_kernel, pps=pps, n_steps=n_steps, n_pages=n_pages, ds=ds,
                          n_heads=n_heads, n_kv=n_kv),
        grid_spec=grid_spec,
        out_shape=jax.ShapeDtypeStruct((db, rows_q, LANES), _bf16),
        compiler_params=_cparams(("arbitrary", "arbitrary")),
        name="sample_attend",
    )(page_table.reshape(-1), q_r, keys, thr, k_new, v_new, cache_k, cache_v)


def _rope_tables(pos, dim):
    half = dim // 2
    inv = ROPE_THETA ** (-jnp.arange(half, dtype=_f32) / half)
    ang = pos.astype(_f32)[:, None] * inv[None, :]
    cos, sin = jnp.cos(ang), jnp.sin(ang)
    return jnp.concatenate([cos, cos], axis=1), jnp.concatenate([-sin, sin], axis=1)


def kernel(x_prompt, x_sample, cache_k, cache_v, cache_ik, state_conv, page_table, meta,
           norm1_g, ffn1_w1, ffn1_w3, ffn1_w2, norm_mix_g, w_in, w_dw, b_dw, conv_ln_g, conv_ln_b,
           w_conv_out, b_conv_out, w_o, w_out, norm2_g, ffn2_w1, ffn2_w3, ffn2_w2, final_g):
    bsz, seq, d = x_prompt.shape
    db, ds, _ = x_sample.shape
    depth, n_pool, page, n_kv, hd = cache_k.shape
    idx_dim = cache_ik.shape[-1]
    n_meta = meta.shape[0]
    width = w_dw.shape[1]
    ch = w_dw.shape[2]
    attn_w = w_o.shape[1]
    n_heads = attn_w // hd
    kv_w = n_kv * hd
    in_w = w_in.shape[2]
    nih = (in_w - 2 * ch - attn_w - 2 * kv_w - idx_dim - 2 * d) // (idx_dim + 1)
    n_pages = page_table.shape[1]
    past = n_pages * page
    assert bsz == 1 and depth == 1 and hd == LANES and idx_dim == LANES and page == LANES
    assert 2 * ch + attn_w + 2 * kv_w + nih * idx_dim + idx_dim + nih + 2 * d == in_w
    assert (n_heads * ds) % 8 == 0 and ds % 8 == 0 and ds <= LANES

    t = n_meta + seq
    tp = _round_up(t, Q_TILE)
    ms = db * ds
    mp = _round_up(tp + ms, FFN_ROW_TILE)
    topk_p = min(MAX_TOPK, seq // 4)
    topk_s = min(MAX_TOPK, (past + ds) // 4)
    assert topk_p <= Q_TILE and topk_s <= LANES * (n_pages + 1)

    h0 = jnp.concatenate([meta, x_prompt[0], jnp.zeros((tp - t, d), _f32),
                          x_sample.reshape(ms, d), jnp.zeros((mp - tp - ms, d), _f32)], axis=0)
    pos = jnp.concatenate([jnp.arange(t, dtype=jnp.int32), jnp.zeros((tp - t,), jnp.int32),
                           jnp.tile(past + jnp.arange(ds, dtype=jnp.int32), db),
                           jnp.zeros((mp - tp - ms,), jnp.int32)])
    cos, sin = _rope_tables(pos, hd)

    vec = lambda a: a.reshape(1, -1).astype(_f32)
    cast = lambda a: a.astype(_bf16)
    offs = np.cumsum([0, ch, ch, attn_w, kv_w, kv_w, nih * idx_dim, idx_dim, nih, d, d])
    wi = w_in[0]
    cols = lambda a, b: cast(wi[:, offs[a]:offs[b]])

    h1, u = _ffn(h0, vec(norm1_g[0]), cast(ffn1_w1[0]), cast(ffn1_w3[0]), cast(ffn1_w2[0]),
                 vec(norm_mix_g[0]), emit_h=True)

    glu = _proj_glu(u, cols(0, 1), cols(1, 2))
    (q,) = _proj(u, cols(2, 3), mode="rope", cos=cos, sin=sin, scale=hd ** -0.5 * LOG2E,
                 out_dtypes=(_bf16,), name="proj_q")
    k32, k16 = _proj(u, cols(3, 4), mode="rope", cos=cos, sin=sin, out_dtypes=(_f32, _bf16), name="proj_k")
    v32, v16 = _proj(u, cols(4, 5), out_dtypes=(_f32, _bf16), name="proj_v")
    (iq,) = _proj(u, cols(5, 6), mode="rope", cos=cos, sin=sin, out_dtypes=(_bf16,), name="proj_iq")
    ik32, ik16 = _proj(u, cols(6, 7), mode="rope", cos=cos, sin=sin, out_dtypes=(_f32, _bf16), name="proj_ik")
    (iw,) = _proj(u, cols(7, 8), scale=(idx_dim ** -0.5) * (nih ** -0.5), name="proj_iw")
    (sg,) = _proj(u, cols(8, 10), mode="sigmoid", out_dtypes=(_bf16,), name="proj_gates")

    ca_p = _conv_prompt(glu, tp, w_dw[0], vec(b_dw[0]), vec(conv_ln_g[0]), vec(conv_ln_b[0]))
    glu_s = glu[tp:tp + ms].reshape(db, ds, ch)
    ext_rows = _round_up(width - 1 + ds, 8)
    ext_s = jnp.concatenate([state_conv[0], glu_s,
                             jnp.zeros((db, ext_rows - (width - 1 + ds), ch), _f32)], axis=1)
    ca_s = _conv_sample(ext_s, w_dw[0], vec(b_dw[0]), vec(conv_ln_g[0]), vec(conv_ln_b[0]), ds)
    ca = jnp.concatenate([ca_p, ca_s.reshape(ms, ch), jnp.zeros((mp - tp - ms, ch), _bf16)], axis=0)

    nc = tp // Q_TILE
    ikt = ik16[:tp].reshape(nc, Q_TILE, idx_dim).transpose(0, 2, 1)
    kt = k16[:tp].reshape(nc, Q_TILE, n_kv, hd).transpose(0, 2, 3, 1)
    vch = v16[:tp].reshape(nc, Q_TILE, kv_w)
    ao_p = _prompt_attention(iq, iw, q, ikt, kt, vch, tp=tp, nih=nih, n_heads=n_heads, n_kv=n_kv,
                             topk=topk_p)

    sl = slice(tp, tp + ms)
    iq_r = iq[sl].reshape(db, ds, nih, idx_dim).transpose(0, 2, 1, 3).reshape(db, nih * ds, idx_dim)
    iw_r = jnp.broadcast_to(iw[sl].reshape(db, ds, nih).transpose(0, 2, 1).reshape(db, nih * ds, 1),
                            (db, nih * ds, LANES))
    q_r = q[sl].reshape(db, ds, n_heads, hd).transpose(0, 2, 1, 3).reshape(db, n_heads * ds, hd)
    pad_keys = lambda a: jnp.pad(a.reshape(db, ds, -1), ((0, 0), (0, LANES - ds), (0, 0)))
    keys_s, thr_s = _sample_index(page_table, iq_r, iw_r, pad_keys(ik16[sl]), cache_ik[0],
                                  ds=ds, nih=nih, topk=topk_s)
    ao_s = _sample_attend(page_table, q_r, keys_s, thr_s, pad_keys(k16[sl]), pad_keys(v16[sl]),
                          cache_k[0].reshape(n_pool, page * n_kv, hd),
                          cache_v[0].reshape(n_pool, page * n_kv, hd),
                          ds=ds, n_heads=n_heads, n_kv=n_kv)
    ao_s = ao_s.reshape(db, n_heads, ds, hd).transpose(0, 2, 1, 3).reshape(ms, attn_w)
    ao = jnp.concatenate([ao_p, ao_s, jnp.zeros((mp - tp - ms, attn_w), _bf16)], axis=0)

    m = _merge(ca, ao, cast(w_conv_out[0]), cast(w_o[0]), vec(b_conv_out[0]), sg)
    h2 = _resid_proj(m, cast(w_out[0]), h1)
    y = _ffn(h2, vec(norm2_g[0]), cast(ffn2_w1[0]), cast(ffn2_w3[0]), cast(ffn2_w2[0]),
             vec(final_g), emit_h=False)

    y_prompt = y[n_meta:t][None]
    y_sample = y[sl].reshape(db, ds, d)
    hist = width - 1
    new_conv_p = glu[t - hist:t][None, None]
    new_conv_s = jnp.concatenate([state_conv[0], glu_s], axis=1)[:, -hist:][None]
    return (y_prompt, y_sample,
            k32[:t].reshape(1, 1, t, n_kv, hd), v32[:t].reshape(1, 1, t, n_kv, hd),
            ik32[:t].reshape(1, 1, t, idx_dim), new_conv_p,
            k32[sl].reshape(1, db, ds, n_kv, hd), v32[sl].reshape(1, db, ds, n_kv, hd),
            ik32[sl].reshape(1, db, ds, idx_dim), new_conv_s)
```

```python
import functools

import numpy as np
import jax
import jax.numpy as jnp
from jax import lax
from jax.experimental import pallas as pl
from jax.experimental.pallas import tpu as pltpu

MAX_TOPK = 256
ROPE_THETA = 10000.0
EPS = 1e-6

LANES = 128
FFN_ROW_TILE = 512
FF_TILE = 512
PROJ_ROW_CAP = 1088
PROJ_COL_TILE = 1024
Q_TILE = 256
PAGES_PER_STEP_INDEX = 32
PAGES_PER_STEP_ATTEND = 8
SEARCH_UNROLL_SAMPLE = 16
SEARCH_ROWS = 256
VMEM_LIMIT = 56 * 1024 * 1024
INT_MIN = -(2 ** 31)
NEG = -1e30
LOG2E = 1.4426950408889634

_bf16 = jnp.bfloat16
_f32 = jnp.float32


def _cparams(sem):
    return pltpu.CompilerParams(dimension_semantics=sem, vmem_limit_bytes=VMEM_LIMIT)


def _dot(a, b):
    return jnp.dot(a, b, preferred_element_type=_f32)


def _dot_nt(a, b):
    return lax.dot_general(a, b, (((1,), (1,)), ((), ())), preferred_element_type=_f32)


def _rms(x, g):
    return x * lax.rsqrt(jnp.mean(x * x, axis=-1, keepdims=True) + EPS) * g


def _sigmoid(x):
    return 1.0 / (1.0 + jnp.exp(-x))


def _round_up(x, m):
    return (x + m - 1) // m * m


def _row_tile(rows, cap):
    return max(t for t in range(16, cap + 1, 16) if rows % t == 0)


def _col_tile(n):
    return PROJ_COL_TILE if n % PROJ_COL_TILE == 0 else (512 if n % 512 == 0 else n)


def _ffn_kernel(h_ref, g_ref, w1_ref, w3_ref, w2_ref, g2_ref, *refs, n_f, emit_h):
    if emit_h:
        h_out_ref, u_out_ref, u_sc, acc_sc = refs
    else:
        y_out_ref, u_sc, acc_sc = refs
    f = pl.program_id(1)

    @pl.when(f == 0)
    def _():
        u_sc[...] = _rms(h_ref[...], g_ref[...]).astype(_bf16)
        acc_sc[...] = jnp.zeros_like(acc_sc)

    u = u_sc[...]
    a = _dot(u, w1_ref[...])
    b = _dot(u, w3_ref[...])
    gate = (a * _sigmoid(a)) * b
    acc_sc[...] += _dot(gate.astype(_bf16), w2_ref[...])

    @pl.when(f == n_f - 1)
    def _():
        hn = h_ref[...] + 0.5 * acc_sc[...]
        un = _rms(hn, g2_ref[...])
        if emit_h:
            h_out_ref[...] = hn
            u_out_ref[...] = un.astype(_bf16)
        else:
            y_out_ref[...] = un


def _ffn(h, g, w1, w3, w2, g2, *, emit_h):
    mp, d = h.shape
    dff = w1.shape[1]
    tm = FFN_ROW_TILE
    tf = min(FF_TILE, dff)
    assert mp % tm == 0 and dff % tf == 0
    n_f = dff // tf
    row = pl.BlockSpec((tm, d), lambda i, f: (i, 0))
    vec = pl.BlockSpec((1, d), lambda i, f: (0, 0))
    if emit_h:
        out_shape = (jax.ShapeDtypeStruct((mp, d), _f32), jax.ShapeDtypeStruct((mp, d), _bf16))
        out_specs = (row, row)
    else:
        out_shape = jax.ShapeDtypeStruct((mp, d), _f32)
        out_specs = row
    return pl.pallas_call(
        functools.partial(_ffn_kernel, n_f=n_f, emit_h=emit_h),
        grid=(mp // tm, n_f),
        in_specs=[row, vec,
                  pl.BlockSpec((d, tf), lambda i, f: (0, f)),
                  pl.BlockSpec((d, tf), lambda i, f: (0, f)),
                  pl.BlockSpec((tf, d), lambda i, f: (f, 0)),
                  vec],
        out_specs=out_specs,
        out_shape=out_shape,
        scratch_shapes=[pltpu.VMEM((tm, d), _bf16), pltpu.VMEM((tm, d), _f32)],
        compiler_params=_cparams(("parallel", "arbitrary")),
        name="ffn_emit_h" if emit_h else "ffn_final",
    )(h, g, w1, w3, w2, g2)


def _weight_cols(w, col0, n, tn):
    if col0 % tn == 0 and (tn % LANES == 0 or (col0 == 0 and n == w.shape[1])):
        return w, col0 // tn
    return w[:, col0:col0 + n], 0


def _load_weight(w_ref, w_sc):
    @pl.when(pl.program_id(1) == 0)
    def _():
        w_sc[...] = w_ref[...].astype(_bf16)
    return w_sc[...]


def _glu_kernel(u_ref, wa_ref, wb_ref, o_ref, wa_sc, wb_sc):
    wa = _load_weight(wa_ref, wa_sc)
    wb = _load_weight(wb_ref, wb_sc)
    u = u_ref[...]
    o_ref[...] = _dot(u, wa) * _sigmoid(_dot(u, wb))


def _proj_glu(u, w, col_a, col_b, n):
    mp, d = u.shape
    tm, tn = _row_tile(mp, PROJ_ROW_CAP), min(512, _col_tile(n))
    wa, ja = _weight_cols(w, col_a, n, tn)
    wb, jb = _weight_cols(w, col_b, n, tn)
    return pl.pallas_call(
        _glu_kernel,
        grid=(n // tn, mp // tm),
        in_specs=[pl.BlockSpec((tm, d), lambda j, i: (i, 0)),
                  pl.BlockSpec((d, tn), lambda j, i: (0, j + ja)),
                  pl.BlockSpec((d, tn), lambda j, i: (0, j + jb))],
        out_specs=pl.BlockSpec((tm, tn), lambda j, i: (i, j)),
        out_shape=jax.ShapeDtypeStruct((mp, n), _f32),
        scratch_shapes=[pltpu.VMEM((d, tn), _bf16), pltpu.VMEM((d, tn), _bf16)],
        compiler_params=_cparams(("parallel", "arbitrary")),
        name="proj_glu",
    )(u, wa, wb)


def _proj_kernel(u_ref, w_ref, *refs, mode, scale, out_dtypes):
    w = _load_weight(w_ref, refs[-1])
    refs = refs[:-1]
    if mode == "rope":
        cos_ref, sin_ref = refs[:2]
        outs = refs[2:]
    else:
        outs = refs
    z = _dot(u_ref[...], w)
    if mode == "rope":
        cos = cos_ref[...]
        sin = sin_ref[...]
        for j in range(z.shape[1] // LANES):
            cs = slice(j * LANES, (j + 1) * LANES)
            x = z[:, cs]
            r = x * cos + pltpu.roll(x, LANES // 2, 1) * sin
            if scale != 1.0:
                r = r * scale
            for o_ref, dt in zip(outs, out_dtypes):
                o_ref[:, cs] = r.astype(dt)
        return
    if mode == "sigmoid":
        z = _sigmoid(z)
    if scale != 1.0:
        z = z * scale
    for o_ref, dt in zip(outs, out_dtypes):
        o_ref[...] = z.astype(dt)


def _proj(u, w, col0, n, *, mode="plain", cos=None, sin=None, scale=1.0, out_dtypes=(_f32,), name="proj"):
    mp, d = u.shape
    tm, tn = _row_tile(mp, PROJ_ROW_CAP), _col_tile(n)
    w, j0 = _weight_cols(w, col0, n, tn)
    in_specs = [pl.BlockSpec((tm, d), lambda j, i: (i, 0)),
                pl.BlockSpec((d, tn), lambda j, i: (0, j + j0))]
    args = [u, w]
    if mode == "rope":
        in_specs += [pl.BlockSpec((tm, LANES), lambda j, i: (i, 0))] * 2
        args += [cos, sin]
    ospec = pl.BlockSpec((tm, tn), lambda j, i: (i, j))
    outs = pl.pallas_call(
        functools.partial(_proj_kernel, mode=mode, scale=scale, out_dtypes=out_dtypes),
        grid=(n // tn, mp // tm),
        in_specs=in_specs,
        out_specs=tuple(ospec for _ in out_dtypes),
        out_shape=tuple(jax.ShapeDtypeStruct((mp, n), dt) for dt in out_dtypes),
        scratch_shapes=[pltpu.VMEM((d, tn), _bf16)],
        compiler_params=_cparams(("parallel", "arbitrary")),
        name=name,
    )(*args)
    return outs


def _ln_swish(c, g, b):
    mu = jnp.mean(c, axis=-1, keepdims=True)
    var = jnp.mean(jnp.square(c - mu), axis=-1, keepdims=True)
    y = (c - mu) * lax.rsqrt(var + EPS) * g + b
    return y * _sigmoid(y)


def _conv_prompt_kernel(cur_ref, halo_ref, w_ref, bdw_ref, g_ref, b_ref, o_ref,
                        ext_sc, sh_sc, wb_sc, c_sc, *, width, tc, halo, cblk):
    i = pl.program_id(0)
    ch = cur_ref.shape[1]
    rb = 16
    ext_sc[pl.ds(0, halo), :] = jnp.where(i > 0, halo_ref[...], 0.0)
    ext_sc[pl.ds(halo, tc), :] = cur_ref[...]
    ext_sc[pl.ds(halo + tc, 8), :] = jnp.zeros((8, ch), _f32)
    lead = halo - (width - 1)
    for cb in range(ch // cblk):
        cs = pl.ds(cb * cblk, cblk)
        for r in range(8):
            sh_sc[r] = ext_sc[pl.ds(r, halo + tc), cs]
        for j in range(width):
            wb_sc[j] = jnp.broadcast_to(w_ref[pl.ds(j, 1), cs], (8, cblk))
        bias = jnp.broadcast_to(bdw_ref[:, cs], (8, cblk))

        def rows(t, _):
            t0 = pl.multiple_of(t * rb, rb)
            accs = [bias] * (rb // 8)
            for j in range(width):
                off = lead + j
                wv = wb_sc[j]
                for a in range(rb // 8):
                    x = sh_sc[off % 8, pl.ds(t0 + a * 8 + (off // 8) * 8, 8), :]
                    accs[a] = accs[a] + x * wv
            for a in range(rb // 8):
                c_sc[pl.ds(t0 + a * 8, 8), cs] = accs[a]
            return 0

        lax.fori_loop(0, tc // rb, rows, 0)
    o_ref[...] = _ln_swish(c_sc[...], g_ref[...], b_ref[...]).astype(_bf16)


def _conv_prompt(glu, tp, w_dw, b_dw, ln_g, ln_b):
    ch = glu.shape[1]
    width = w_dw.shape[0]
    tc = Q_TILE
    halo = 32
    assert width - 1 <= halo and tp % tc == 0 and tc % halo == 0
    cblk = min(512, ch)
    vec = pl.BlockSpec((1, ch), lambda i: (0, 0))
    return pl.pallas_call(
        functools.partial(_conv_prompt_kernel, width=width, tc=tc, halo=halo, cblk=cblk),
        grid=(tp // tc,),
        in_specs=[pl.BlockSpec((tc, ch), lambda i: (i, 0)),
                  pl.BlockSpec((halo, ch), lambda i: (jnp.maximum(i * (tc // halo) - 1, 0), 0)),
                  pl.BlockSpec((width, ch), lambda i: (0, 0)),
                  vec, vec, vec],
        out_specs=pl.BlockSpec((tc, ch), lambda i: (i, 0)),
        out_shape=jax.ShapeDtypeStruct((tp, ch), _bf16),
        scratch_shapes=[pltpu.VMEM((halo + tc + 8, ch), _f32),
                        pltpu.VMEM((8, halo + tc, cblk), _f32),
                        pltpu.VMEM((width, 8, cblk), _f32),
                        pltpu.VMEM((tc, ch), _f32)],
        compiler_params=_cparams(("parallel",)),
        name="conv_prompt",
    )(glu, glu, w_dw, b_dw, ln_g, ln_b)


def _conv_sample_kernel(ext_ref, w_ref, bdw_ref, g_ref, b_ref, o_ref, *, width, ds):
    ch = ext_ref.shape[2]
    acc = jnp.broadcast_to(bdw_ref[...], (ds, ch))
    for j in range(width):
        acc = acc + ext_ref[0, pl.ds(j, ds), :] * w_ref[pl.ds(j, 1), :]
    o_ref[0] = _ln_swish(acc, g_ref[...], b_ref[...]).astype(_bf16)


def _conv_sample(ext, w_dw, b_dw, ln_g, ln_b, ds):
    db, rows, ch = ext.shape
    width = w_dw.shape[0]
    vec = pl.BlockSpec((1, ch), lambda b: (0, 0))
    return pl.pallas_call(
        functools.partial(_conv_sample_kernel, width=width, ds=ds),
        grid=(db,),
        in_specs=[pl.BlockSpec((1, rows, ch), lambda b: (b, 0, 0)),
                  pl.BlockSpec((width, ch), lambda b: (0, 0)),
                  vec, vec, vec],
        out_specs=pl.BlockSpec((1, ds, ch), lambda b: (b, 0, 0)),
        out_shape=jax.ShapeDtypeStruct((db, ds, ch), _bf16),
        compiler_params=_cparams(("parallel",)),
        name="conv_sample",
    )(ext, w_dw, b_dw, ln_g, ln_b)


def _merge_kernel(ca_ref, ao_ref, wco_ref, wo_ref, bco_ref, sgc_ref, sga_ref, m_ref, wco_sc, wo_sc):
    wco = _load_weight(wco_ref, wco_sc)
    wo = _load_weight(wo_ref, wo_sc)
    yc = _dot(ca_ref[...], wco) + bco_ref[...]
    ya = _dot(ao_ref[...], wo)
    m_ref[...] = (sgc_ref[...].astype(_f32) * yc + sga_ref[...].astype(_f32) * ya).astype(_bf16)


def _merge(ca, ao, wco, wo, bco, sg):
    mp, ch = ca.shape
    aw = ao.shape[1]
    d = wco.shape[1]
    tm = _row_tile(mp, PROJ_ROW_CAP)
    tn = min(512, d)
    nj = d // tn
    once = pl.Buffered(1)
    return pl.pallas_call(
        _merge_kernel,
        grid=(nj, mp // tm),
        in_specs=[pl.BlockSpec((tm, ch), lambda j, i: (i, 0)),
                  pl.BlockSpec((tm, aw), lambda j, i: (i, 0)),
                  pl.BlockSpec((ch, tn), lambda j, i: (0, j), pipeline_mode=once),
                  pl.BlockSpec((aw, tn), lambda j, i: (0, j), pipeline_mode=once),
                  pl.BlockSpec((1, tn), lambda j, i: (0, j)),
                  pl.BlockSpec((tm, tn), lambda j, i: (i, j)),
                  pl.BlockSpec((tm, tn), lambda j, i: (i, j + nj))],
        out_specs=pl.BlockSpec((tm, tn), lambda j, i: (i, j)),
        out_shape=jax.ShapeDtypeStruct((mp, d), _bf16),
        scratch_shapes=[pltpu.VMEM((ch, tn), _bf16), pltpu.VMEM((aw, tn), _bf16)],
        compiler_params=_cparams(("parallel", "arbitrary")),
        name="merge",
    )(ca, ao, wco, wo, bco, sg, sg)


def _resid_kernel(m_ref, w_ref, h_ref, o_ref, w_sc):
    o_ref[...] = h_ref[...] + _dot(m_ref[...], _load_weight(w_ref, w_sc))


def _resid_proj(m, w, h):
    mp, d = m.shape
    n = w.shape[1]
    tm, tn = _row_tile(mp, PROJ_ROW_CAP), min(512, _col_tile(n))
    return pl.pallas_call(
        _resid_kernel,
        grid=(n // tn, mp // tm),
        in_specs=[pl.BlockSpec((tm, d), lambda j, i: (i, 0)),
                  pl.BlockSpec((d, tn), lambda j, i: (0, j)),
                  pl.BlockSpec((tm, tn), lambda j, i: (i, j))],
        out_specs=pl.BlockSpec((tm, tn), lambda j, i: (i, j)),
        out_shape=jax.ShapeDtypeStruct((mp, n), _f32),
        scratch_shapes=[pltpu.VMEM((d, tn), _bf16)],
        compiler_params=_cparams(("parallel", "arbitrary")),
        name="resid_proj",
    )(m, w, h)


def _order_key(s):
    b = lax.bitcast_convert_type(s, jnp.int32)
    return b ^ (lax.shift_right_arithmetic(b, 31) & jnp.int32(0x7FFFFFFF))


def _bit_search(count_ge, rows, n_bits, k, lowest):
    lowest = jnp.int32(lowest)

    def body(b, t_u):
        bit = lax.shift_left(jnp.int32(1), jnp.int32(n_bits - 1) - jnp.asarray(b, jnp.int32))
        cand = t_u | bit
        cnt = count_ge(cand + lowest)
        return jnp.where(cnt >= k, cand, t_u)

    t_u = lax.fori_loop(0, n_bits, body, jnp.zeros((rows, 1), jnp.int32))
    return t_u + lowest


def _kth_key_packed(load, pk_ref, scan, rows, width, k):
    halves = width // LANES
    i16 = jnp.int16
    half_min, half_max = -(2 ** 15), 2 ** 15 - 1

    def count16(th):
        th16 = th.astype(i16)

        def one(c, acc):
            w = jnp.where(pk_ref[c] >= th16, i16(1), i16(0))
            for j in range(halves):
                acc = acc + w[:, j * LANES:(j + 1) * LANES]
            return acc
        acc = scan(one, jnp.zeros((rows, LANES), i16))
        return jnp.sum(acc.astype(_f32), axis=1, keepdims=True)

    def fill_top(c, carry):
        pk_ref[c] = lax.shift_right_arithmetic(load(c), 16).astype(i16)
        return carry
    scan(fill_top, 0)
    top = _bit_search(count16, rows, 16, k, half_min)
    above = jnp.where(top >= half_max, 0.0, count16(jnp.minimum(top + 1, half_max)))

    def fill_low(c, carry):
        x = load(c)
        low = (x & jnp.int32(0xFFFF)) + half_min
        pk_ref[c] = jnp.where(lax.shift_right_arithmetic(x, 16) == top, low, half_min).astype(i16)
        return carry
    scan(fill_low, 0)
    low = _bit_search(count16, rows, 16, k - above, half_min)
    return lax.shift_left(top, 16) | (low - half_min)


def _select_threshold(load, store, n_iters, unroll, rows, width, k, idx_bits, pk_ref=None):
    halves = width // LANES

    def scan(fn, init):
        def body(it, carry):
            for u in range(unroll):
                carry = fn(it * unroll + u, carry)
            return carry
        return lax.fori_loop(0, n_iters, body, init)

    def count(pred):
        def one(c, acc):
            m = pred(load(c), c)
            for j in range(halves):
                acc = acc + jnp.where(m[:, j * LANES:(j + 1) * LANES], 1.0, 0.0)
            return acc
        return jnp.sum(scan(one, jnp.zeros((rows, LANES), _f32)), axis=1, keepdims=True)

    kf = jnp.float32(k)
    if pk_ref is None:
        t = _bit_search(lambda th: count(lambda x, c: x >= th), rows, 32, kf, INT_MIN)
    else:
        t = _kth_key_packed(load, pk_ref, scan, rows, width, kf)
    t = jnp.maximum(t, jnp.int32(INT_MIN + 1))
    n_ge = count(lambda x, c: x >= t)

    @pl.when(jnp.max(n_ge) > kf)
    def _():
        n_gt = count(lambda x, c: x > t)
        need = kf - n_gt
        lane = lax.broadcasted_iota(jnp.int32, (rows, width), 1)

        def ties_below(j):
            return count(lambda x, c: (x == t) & ((lane + c * width) < j))

        j_cut = _bit_search(lambda j: need - ties_below(j) + (kf - 1.0), rows, idx_bits, kf, 0)
        surplus = n_ge > kf

        def rewrite(c, carry):
            x = load(c)
            drop = (x == t) & ((lane + c * width) > j_cut) & surplus
            store(c, jnp.where(drop, jnp.int32(INT_MIN), x))
            return carry
        scan(rewrite, 0)

    return t


def _prompt_attn_kernel(iq_ref, iw_ref, q_ref, ikt_ref, kt_ref, v_ref, o_ref,
                        key_sc, pk_sc, thr_sc, wb_sc, m_sc, acc_sc,
                        *, nih, n_heads, n_kv, topk, idx_bits, sub):
    i = pl.program_id(0)
    tq = q_ref.shape[0]
    group = n_heads // n_kv
    unroll = 2

    for h in range(nih):
        wb_sc[h] = jnp.broadcast_to(iw_ref[:, h:h + 1], (tq, LANES))

    def scores(c):
        rhs = ikt_ref[c]
        halves = [jnp.zeros((tq, LANES), _f32) for _ in range(tq // LANES)]
        for h in range(nih):
            d = _dot(iq_ref[:, h * LANES:(h + 1) * LANES], rhs)
            w = wb_sc[h]
            for j in range(len(halves)):
                halves[j] = halves[j] + jnp.maximum(d[:, j * LANES:(j + 1) * LANES], 0.0) * w
        return _order_key(jnp.concatenate(halves, axis=1))

    def score_body(c, _):
        key_sc[c] = scores(c)
        return 0

    lax.fori_loop(0, i, score_body, 0)
    row = lax.broadcasted_iota(jnp.int32, (tq, tq), 0)
    col = lax.broadcasted_iota(jnp.int32, (tq, tq), 1)
    key_sc[i] = jnp.where(col <= row, scores(i), jnp.int32(INT_MIN))
    key_sc[i + 1] = jnp.full((tq, tq), INT_MIN, jnp.int32)
    n_iters = lax.shift_right_logical(i + unroll, 1)

    for rb in range(tq // sub):
        rs = pl.ds(rb * sub, sub)
        t = _select_threshold(lambda c: key_sc[c, rs, :],
                              lambda c, x: key_sc.__setitem__((c, rs, slice(None)), x),
                              n_iters, unroll, sub, tq, topk, idx_bits, pk_ref=pk_sc)
        thr_sc[rs, :] = jnp.broadcast_to(t, (sub, LANES))

    thr = jnp.concatenate([thr_sc[...]] * (tq // LANES), axis=1)
    m_sc[...] = jnp.full(m_sc.shape, NEG, _f32)
    acc_sc[...] = jnp.zeros_like(acc_sc)
    ones = jnp.ones((tq, LANES), _bf16)

    def attend(c, _):
        bias = jnp.where(key_sc[c] >= thr, 0.0, NEG)
        vc = v_ref[c]

        for n in range(n_kv):
            kt = kt_ref[c, n]
            vn1 = jnp.concatenate([vc[:, n * LANES:(n + 1) * LANES], ones], axis=1)
            for g in range(group):
                h = n * group + g
                lg = _dot(q_ref[:, h * LANES:(h + 1) * LANES], kt) + bias
                m_prev = m_sc[h]
                m_new = jnp.maximum(m_prev, jnp.max(lg, axis=1, keepdims=True))
                alpha = jnp.exp2(m_prev - m_new)
                p = jnp.exp2(lg - jnp.concatenate([m_new] * (tq // LANES), axis=1))
                acc_sc[h] = jnp.concatenate([alpha, alpha], axis=1) * acc_sc[h] + _dot(p.astype(_bf16), vn1)
                m_sc[h] = m_new
        return 0

    lax.fori_loop(0, i + 1, attend, 0)
    for h in range(n_heads):
        a = acc_sc[h]
        o_ref[:, h * LANES:(h + 1) * LANES] = (a[:, :LANES] / a[:, LANES:]).astype(_bf16)


def _prompt_attention(iq, iw, q, ikt, kt, v, *, tp, nih, n_heads, n_kv, topk):
    tq = Q_TILE
    nc = tp // tq
    resident = lambda shape: pl.BlockSpec(shape, lambda i: (0,) * len(shape),
                                          pipeline_mode=pl.Buffered(1))
    idx_bits = max(1, int(np.ceil(np.log2(tp + 1))))
    sub = min(SEARCH_ROWS, tq)
    return pl.pallas_call(
        functools.partial(_prompt_attn_kernel, nih=nih, n_heads=n_heads, n_kv=n_kv, topk=topk,
                          idx_bits=idx_bits, sub=sub),
        grid=(nc,),
        in_specs=[pl.BlockSpec((tq, nih * LANES), lambda i: (i, 0)),
                  pl.BlockSpec((tq, nih), lambda i: (i, 0)),
                  pl.BlockSpec((tq, n_heads * LANES), lambda i: (i, 0)),
                  resident((nc, LANES, tq)),
                  resident((nc, n_kv, LANES, tq)),
                  resident((nc, tq, n_kv * LANES))],
        out_specs=pl.BlockSpec((tq, n_heads * LANES), lambda i: (i, 0)),
        out_shape=jax.ShapeDtypeStruct((tp, n_heads * LANES), _bf16),
        scratch_shapes=[pltpu.VMEM((nc + 1, tq, tq), jnp.int32),
                        pltpu.VMEM((nc + 1, sub, tq), jnp.int16),
                        pltpu.VMEM((tq, LANES), jnp.int32),
                        pltpu.VMEM((nih, tq, LANES), _f32),
                        pltpu.VMEM((n_heads, tq, LANES), _f32),
                        pltpu.VMEM((n_heads, tq, 2 * LANES), _f32)],
        compiler_params=_cparams(("arbitrary",)),
        name="prompt_attention",
    )(iq, iw, q, ikt, kt, v)


def _ring_copies(pt_ref, streams, sem_ref, pps, g):
    slot = lax.rem(g, 2)
    return [pltpu.make_async_copy(cache.at[pt_ref[g * pps + j]], buf.at[slot, j], sem_ref.at[slot])
            for j in range(pps) for cache, buf in streams]


def _ring_step(pt_ref, streams, sem_ref, pps, g, n_total):
    def request(step):
        for n, cp in enumerate(_ring_copies(pt_ref, streams, sem_ref, pps, step)):
            cp.start(priority=n % 2)

    @pl.when(g == 0)
    def _():
        request(g)

    @pl.when(g + 1 < n_total)
    def _():
        request(g + 1)

    for cp in _ring_copies(pt_ref, streams, sem_ref, pps, g):
        cp.wait()


def _sample_index_kernel(pt_ref, iq_ref, iw_ref, ikn_ref, cache_ref, key_ref, thr_ref, buf, sem,
                         *, pps, n_steps, n_pages, n_chunks, ds, nih, topk, idx_bits):
    s = pl.program_id(1)
    g = pl.program_id(0) * n_steps + s
    _ring_step(pt_ref, [(cache_ref, buf)], sem, pps, g, pl.num_programs(0) * n_steps)
    slot = lax.rem(g, 2)

    def index_keys(ik_rows):
        n_keys = ik_rows.shape[0]
        w = jnp.concatenate([iw_ref[0]] * (n_keys // LANES), axis=1)
        x = jnp.maximum(_dot_nt(iq_ref[0], ik_rows), 0.0) * w
        return _order_key(jnp.sum(x.reshape(nih, ds, n_keys), axis=0))

    pair = 2 if pps % 2 == 0 else 1
    for j in range(0, pps, pair):
        pages = jnp.concatenate([buf[slot, j + a].astype(_bf16) for a in range(pair)], axis=0)
        keys = index_keys(pages)
        for a in range(pair):
            key_ref[0, s * pps + j + a] = keys[:, a * LANES:(a + 1) * LANES]

    @pl.when(s == n_steps - 1)
    def _():
        qi = lax.broadcasted_iota(jnp.int32, (ds, LANES), 0)
        ki = lax.broadcasted_iota(jnp.int32, (ds, LANES), 1)
        key_ref[0, n_pages] = jnp.where(ki <= qi, index_keys(ikn_ref[0]), jnp.int32(INT_MIN))
        for c in range(n_pages + 1, n_chunks):
            key_ref[0, c] = jnp.full((ds, LANES), INT_MIN, jnp.int32)
        t = _select_threshold(lambda c: key_ref[0, c],
                              lambda c, x: key_ref.__setitem__((0, c), x),
                              n_chunks // SEARCH_UNROLL_SAMPLE, SEARCH_UNROLL_SAMPLE,
                              ds, LANES, topk, idx_bits)
        thr_ref[0] = jnp.broadcast_to(t, (ds, LANES))


def _sample_index(page_table, iq_r, iw_r, ik_new, cache_ik, *, ds, nih, topk):
    db, n_pages = page_table.shape
    pps = min(PAGES_PER_STEP_INDEX, n_pages)
    assert n_pages % pps == 0
    n_steps = n_pages // pps
    page = cache_ik.shape[1]
    n_chunks = _round_up(n_pages + 1, SEARCH_UNROLL_SAMPLE)
    rows_i = nih * ds
    idx_bits = max(1, int(np.ceil(np.log2(n_chunks * LANES + 1))))
    per_b = lambda shape: pl.BlockSpec((1,) + shape, lambda b, s, pt: (b,) + (0,) * len(shape))

    grid_spec = pltpu.PrefetchScalarGridSpec(
        num_scalar_prefetch=1,
        grid=(db, n_steps),
        in_specs=[per_b((rows_i, LANES)), per_b((rows_i, LANES)), per_b((LANES, LANES)),
                  pl.BlockSpec(memory_space=pl.ANY)],
        out_specs=(per_b((n_chunks, ds, LANES)), per_b((ds, LANES))),
        scratch_shapes=[pltpu.VMEM((2, pps, page, LANES), _f32), pltpu.SemaphoreType.DMA((2,))],
    )
    return pl.pallas_call(
        functools.partial(_sample_index_kernel, pps=pps, n_steps=n_steps, n_pages=n_pages,
                          n_chunks=n_chunks, ds=ds, nih=nih, topk=topk, idx_bits=idx_bits),
        grid_spec=grid_spec,
        out_shape=(jax.ShapeDtypeStruct((db, n_chunks, ds, LANES), jnp.int32),
                   jax.ShapeDtypeStruct((db, ds, LANES), jnp.int32)),
        compiler_params=_cparams(("arbitrary", "arbitrary")),
        name="sample_index",
    )(page_table.reshape(-1), iq_r, iw_r, ik_new, cache_ik)


def _sample_attend_kernel(pt_ref, q_ref, key_ref, thr_ref, kn_ref, vn_ref, ck_ref, cv_ref, o_ref,
                          kbuf, vbuf, sem, m_sc, l_sc, acc_sc,
                          *, pps, n_steps, n_pages, ds, n_heads, n_kv):
    s = pl.program_id(1)
    g = pl.program_id(0) * n_steps + s
    _ring_step(pt_ref, [(ck_ref, kbuf), (cv_ref, vbuf)], sem, pps, g, pl.num_programs(0) * n_steps)
    slot = lax.rem(g, 2)
    group = n_heads // n_kv
    rows_g = group * ds
    page = kbuf.shape[2] // n_kv

    @pl.when(s == 0)
    def _():
        m_sc[...] = jnp.full(m_sc.shape, NEG, _f32)
        l_sc[...] = jnp.zeros_like(l_sc)
        acc_sc[...] = jnp.zeros_like(acc_sc)

    def attend(chunk_ids, k_of, v_of):
        thr = thr_ref[0]
        bias = jnp.concatenate([jnp.where(key_ref[0, c] >= thr, 0.0, NEG) for c in chunk_ids], axis=1)
        bias = jnp.concatenate([bias] * group, axis=0)
        m_prev, l_prev, acc_prev = m_sc[...], l_sc[...], acc_sc[...]
        heads = [slice(n * rows_g, (n + 1) * rows_g) for n in range(n_kv)]
        lgs = [_dot_nt(q_ref[0, rs, :],
                       jnp.concatenate([k_of(j, n) for j in range(len(chunk_ids))], axis=0)) + bias
               for n, rs in enumerate(heads)]
        m_out = [jnp.maximum(m_prev[rs], jnp.max(lg, axis=1, keepdims=True)) for lg, rs in zip(lgs, heads)]
        ps = [jnp.exp2(lg - m_new[:, :1]) for lg, m_new in zip(lgs, m_out)]
        pvs = [_dot(p.astype(_bf16), jnp.concatenate([v_of(j, n) for j in range(len(chunk_ids))], axis=0))
               for n, p in enumerate(ps)]
        l_out, acc_out = [], []
        for rs, m_new, p, pv in zip(heads, m_out, ps, pvs):
            alpha = jnp.exp2(m_prev[rs] - m_new)
            l_out.append(alpha * l_prev[rs] + jnp.sum(p, axis=1, keepdims=True))
            acc_out.append(alpha * acc_prev[rs] + pv)
        m_sc[...] = jnp.concatenate(m_out, axis=0)
        l_sc[...] = jnp.concatenate(l_out, axis=0)
        acc_sc[...] = jnp.concatenate(acc_out, axis=0)

    def head_rows(buf, j, n):
        return buf[slot, j, pl.ds(n, page, stride=n_kv), :].astype(_bf16)

    attend([s * pps + j for j in range(pps)],
           lambda j, n: head_rows(kbuf, j, n), lambda j, n: head_rows(vbuf, j, n))

    @pl.when(s == n_steps - 1)
    def _():
        attend([n_pages],
               lambda j, n: kn_ref[0, :, n * LANES:(n + 1) * LANES],
               lambda j, n: vn_ref[0, :, n * LANES:(n + 1) * LANES])
        o_ref[0] = (acc_sc[...] / l_sc[...]).astype(_bf16)


def _sample_attend(page_table, q_r, keys, thr, k_new, v_new, cache_k, cache_v, *, ds, n_heads, n_kv):
    db, n_pages = page_table.shape
    pps = min(PAGES_PER_STEP_ATTEND, n_pages)
    assert n_pages % pps == 0
    n_steps = n_pages // pps
    n_chunks = keys.shape[1]
    rows_q = n_heads * ds
    kvw = n_kv * LANES
    page_rows = cache_k.shape[1]
    per_b = lambda shape: pl.BlockSpec((1,) + shape, lambda b, s, pt: (b,) + (0,) * len(shape))
    hbm = pl.BlockSpec(memory_space=pl.ANY)

    grid_spec = pltpu.PrefetchScalarGridSpec(
        num_scalar_prefetch=1,
        grid=(db, n_steps),
        in_specs=[per_b((rows_q, LANES)), per_b((n_chunks, ds, LANES)), per_b((ds, LANES)),
                  per_b((LANES, kvw)), per_b((LANES, kvw)), hbm, hbm],
        out_specs=per_b((rows_q, LANES)),
        scratch_shapes=[pltpu.VMEM((2, pps, page_rows, LANES), _f32),
                        pltpu.VMEM((2, pps, page_rows, LANES), _f32),
                        pltpu.SemaphoreType.DMA((2,)),
                        pltpu.VMEM((rows_q, LANES), _f32),
                        pltpu.VMEM((rows_q, LANES), _f32),
                        pltpu.VMEM((rows_q, LANES), _f32)],
    )
    return pl.pallas_call(
        functools.partial(_sample_attend_kernel, pps=pps, n_steps=n_steps, n_pages=n_pages, ds=ds,
                          n_heads=n_heads, n_kv=n_kv),
        grid_spec=grid_spec,
        out_shape=jax.ShapeDtypeStruct((db, rows_q, LANES), _bf16),
        compiler_params=_cparams(("arbitrary", "arbitrary")),
        name="sample_attend",
    )(page_table.reshape(-1), q_r, keys, thr, k_new, v_new, cache_k, cache_v)


def _rope_tables(pos, dim):
    half = dim // 2
    inv = ROPE_THETA ** (-jnp.arange(half, dtype=_f32) / half)
    ang = pos.astype(_f32)[:, None] * inv[None, :]
    cos, sin = jnp.cos(ang), jnp.sin(ang)
    return jnp.concatenate([cos, cos], axis=1), jnp.concatenate([-sin, sin], axis=1)


def kernel(x_prompt, x_sample, cache_k, cache_v, cache_ik, state_conv, page_table, meta,
           norm1_g, ffn1_w1, ffn1_w3, ffn1_w2, norm_mix_g, w_in, w_dw, b_dw, conv_ln_g, conv_ln_b,
           w_conv_out, b_conv_out, w_o, w_out, norm2_g, ffn2_w1, ffn2_w3, ffn2_w2, final_g):
    bsz, seq, d = x_prompt.shape
    db, ds, _ = x_sample.shape
    depth, n_pool, page, n_kv, hd = cache_k.shape
    idx_dim = cache_ik.shape[-1]
    n_meta = meta.shape[0]
    width = w_dw.shape[1]
    ch = w_dw.shape[2]
    attn_w = w_o.shape[1]
    n_heads = attn_w // hd
    kv_w = n_kv * hd
    in_w = w_in.shape[2]
    nih = (in_w - 2 * ch - attn_w - 2 * kv_w - idx_dim - 2 * d) // (idx_dim + 1)
    n_pages = page_table.shape[1]
    past = n_pages * page
    assert bsz == 1 and depth == 1 and hd == LANES and idx_dim == LANES and page == LANES
    assert 2 * ch + attn_w + 2 * kv_w + nih * idx_dim + idx_dim + nih + 2 * d == in_w
    assert (n_heads * ds) % 8 == 0 and ds % 8 == 0 and ds <= LANES

    t = n_meta + seq
    tp = _round_up(t, Q_TILE)
    ms = db * ds
    mp = _round_up(tp + ms, FFN_ROW_TILE)
    topk_p = min(MAX_TOPK, seq // 4)
    topk_s = min(MAX_TOPK, (past + ds) // 4)
    assert topk_p <= Q_TILE and topk_s <= LANES * (n_pages + 1)

    h0 = jnp.concatenate([meta, x_prompt[0], jnp.zeros((tp - t, d), _f32),
                          x_sample.reshape(ms, d), jnp.zeros((mp - tp - ms, d), _f32)], axis=0)
    pos = jnp.concatenate([jnp.arange(t, dtype=jnp.int32), jnp.zeros((tp - t,), jnp.int32),
                           jnp.tile(past + jnp.arange(ds, dtype=jnp.int32), db),
                           jnp.zeros((mp - tp - ms,), jnp.int32)])
    cos, sin = _rope_tables(pos, hd)

    vec = lambda a: a.reshape(1, -1).astype(_f32)
    cast = lambda a: a.astype(_bf16)
    offs = [int(o) for o in np.cumsum([0, ch, ch, attn_w, kv_w, kv_w, nih * idx_dim, idx_dim, nih, d, d])]
    wi = w_in[0]
    rope = dict(mode="rope", cos=cos, sin=sin)

    h1, u = _ffn(h0, vec(norm1_g[0]), cast(ffn1_w1[0]), cast(ffn1_w3[0]), cast(ffn1_w2[0]),
                 vec(norm_mix_g[0]), emit_h=True)

    glu = _proj_glu(u, wi, offs[0], offs[1], ch)
    (q,) = _proj(u, wi, offs[2], attn_w, scale=hd ** -0.5 * LOG2E, out_dtypes=(_bf16,), name="proj_q", **rope)
    k32, k16 = _proj(u, wi, offs[3], kv_w, out_dtypes=(_f32, _bf16), name="proj_k", **rope)
    v32, v16 = _proj(u, wi, offs[4], kv_w, out_dtypes=(_f32, _bf16), name="proj_v")
    (iq,) = _proj(u, wi, offs[5], nih * idx_dim, out_dtypes=(_bf16,), name="proj_iq", **rope)
    ik32, ik16 = _proj(u, wi, offs[6], idx_dim, out_dtypes=(_f32, _bf16), name="proj_ik", **rope)
    (iw,) = _proj(u, wi, offs[7], nih, scale=(idx_dim ** -0.5) * (nih ** -0.5), name="proj_iw")
    (sg,) = _proj(u, wi, offs[8], 2 * d, mode="sigmoid", out_dtypes=(_bf16,), name="proj_gates")

    ca_p = _conv_prompt(glu, tp, w_dw[0], vec(b_dw[0]), vec(conv_ln_g[0]), vec(conv_ln_b[0]))
    glu_s = glu[tp:tp + ms].reshape(db, ds, ch)
    ext_rows = _round_up(width - 1 + ds, 8)
    ext_s = jnp.concatenate([state_conv[0], glu_s,
                             jnp.zeros((db, ext_rows - (width - 1 + ds), ch), _f32)], axis=1)
    ca_s = _conv_sample(ext_s, w_dw[0], vec(b_dw[0]), vec(conv_ln_g[0]), vec(conv_ln_b[0]), ds)
    ca = jnp.concatenate([ca_p, ca_s.reshape(ms, ch), jnp.zeros((mp - tp - ms, ch), _bf16)], axis=0)

    nc = tp // Q_TILE
    ikt = ik16[:tp].reshape(nc, Q_TILE, idx_dim).transpose(0, 2, 1)
    kt = k16[:tp].reshape(nc, Q_TILE, n_kv, hd).transpose(0, 2, 3, 1)
    vch = v16[:tp].reshape(nc, Q_TILE, kv_w)
    ao_p = _prompt_attention(iq, iw, q, ikt, kt, vch, tp=tp, nih=nih, n_heads=n_heads, n_kv=n_kv,
                             topk=topk_p)

    sl = slice(tp, tp + ms)
    iq_r = iq[sl].reshape(db, ds, nih, idx_dim).transpose(0, 2, 1, 3).reshape(db, nih * ds, idx_dim)
    iw_r = jnp.broadcast_to(iw[sl].reshape(db, ds, nih).transpose(0, 2, 1).reshape(db, nih * ds, 1),
                            (db, nih * ds, LANES))
    q_r = q[sl].reshape(db, ds, n_heads, hd).transpose(0, 2, 1, 3).reshape(db, n_heads * ds, hd)
    pad_keys = lambda a: jnp.pad(a.reshape(db, ds, -1), ((0, 0), (0, LANES - ds), (0, 0)))
    keys_s, thr_s = _sample_index(page_table, iq_r, iw_r, pad_keys(ik16[sl]), cache_ik[0],
                                  ds=ds, nih=nih, topk=topk_s)
    ao_s = _sample_attend(page_table, q_r, keys_s, thr_s, pad_keys(k16[sl]), pad_keys(v16[sl]),
                          cache_k[0].reshape(n_pool, page * n_kv, hd),
                          cache_v[0].reshape(n_pool, page * n_kv, hd),
                          ds=ds, n_heads=n_heads, n_kv=n_kv)
    ao_s = ao_s.reshape(db, n_heads, ds, hd).transpose(0, 2, 1, 3).reshape(ms, attn_w)
    ao = jnp.concatenate([ao_p, ao_s, jnp.zeros((mp - tp - ms, attn_w), _bf16)], axis=0)

    m = _merge(ca, ao, w_conv_out[0], w_o[0], vec(b_conv_out[0]), sg)
    h2 = _resid_proj(m, w_out[0], h1)
    y = _ffn(h2, vec(norm2_g[0]), cast(ffn2_w1[0]), cast(ffn2_w3[0]), cast(ffn2_w2[0]),
             vec(final_g), emit_h=False)

    y_prompt = y[n_meta:t][None]
    y_sample = y[sl].reshape(db, ds, d)
    hist = width - 1
    new_conv_p = glu[t - hist:t][None, None]
    new_conv_s = jnp.concatenate([state_conv[0], glu_s], axis=1)[:, -hist:][None]
    return (y_prompt, y_sample,
            k32[:t].reshape(1, 1, t, n_kv, hd), v32[:t].reshape(1, 1, t, n_kv, hd),
            ik32[:t].reshape(1, 1, t, idx_dim), new_conv_p,
            k32[sl].reshape(1, db, ds, n_kv, hd), v32[sl].reshape(1, db, ds, n_kv, hd),
            ik32[sl].reshape(1, db, ds, idx_dim), new_conv_s)
```

```python
import functools

import numpy as np
import jax
import jax.numpy as jnp
from jax import lax
from jax.experimental import pallas as pl
from jax.experimental.pallas import tpu as pltpu

MAX_TOPK = 256
ROPE_THETA = 10000.0
EPS = 1e-6

LANES = 128
FFN_ROW_TILE = 512
FF_TILE = 512
PROJ_ROW_CAP = 1088
PROJ_COL_TILE = 1024
Q_TILE = 256
PAGES_PER_STEP_INDEX = 32
PAGES_PER_STEP_ATTEND = 8
RING_SLOTS = 4
SEARCH_UNROLL_SAMPLE = 16
SEARCH_ROWS = 128
VMEM_LIMIT = 56 * 1024 * 1024
INT_MIN = -(2 ** 31)
NEG = -1e30
LOG2E = 1.4426950408889634

_bf16 = jnp.bfloat16
_f32 = jnp.float32


def _cparams(sem):
    return pltpu.CompilerParams(dimension_semantics=sem, vmem_limit_bytes=VMEM_LIMIT)


def _dot(a, b):
    return jnp.dot(a, b, preferred_element_type=_f32)


def _dot_nt(a, b):
    return lax.dot_general(a, b, (((1,), (1,)), ((), ())), preferred_element_type=_f32)


def _rms(x, g):
    return x * lax.rsqrt(jnp.mean(x * x, axis=-1, keepdims=True) + EPS) * g


def _sigmoid(x):
    return 1.0 / (1.0 + jnp.exp(-x))


def _round_up(x, m):
    return (x + m - 1) // m * m


def _row_tile(rows, cap):
    return max(t for t in range(16, cap + 1, 16) if rows % t == 0)


def _col_tile(n):
    return PROJ_COL_TILE if n % PROJ_COL_TILE == 0 else (512 if n % 512 == 0 else n)


def _ffn_kernel(h_ref, g_ref, w1_ref, w3_ref, w2_ref, g2_ref, *refs, n_f, emit_h):
    if emit_h:
        h_out_ref, u_out_ref, u_sc, acc_sc = refs
    else:
        y_out_ref, u_sc, acc_sc = refs
    f = pl.program_id(1)

    @pl.when(f == 0)
    def _():
        u_sc[...] = _rms(h_ref[...], g_ref[...]).astype(_bf16)
        acc_sc[...] = jnp.zeros_like(acc_sc)

    u = u_sc[...]
    a = _dot(u, w1_ref[...])
    b = _dot(u, w3_ref[...])
    gate = (a * _sigmoid(a)) * b
    acc_sc[...] += _dot(gate.astype(_bf16), w2_ref[...])

    @pl.when(f == n_f - 1)
    def _():
        hn = h_ref[...] + 0.5 * acc_sc[...]
        un = _rms(hn, g2_ref[...])
        if emit_h:
            h_out_ref[...] = hn
            u_out_ref[...] = un.astype(_bf16)
        else:
            y_out_ref[...] = un


def _ffn(h, g, w1, w3, w2, g2, *, emit_h):
    mp, d = h.shape
    dff = w1.shape[1]
    tm = FFN_ROW_TILE
    tf = min(FF_TILE, dff)
    assert mp % tm == 0 and dff % tf == 0
    n_f = dff // tf
    row = pl.BlockSpec((tm, d), lambda i, f: (i, 0))
    vec = pl.BlockSpec((1, d), lambda i, f: (0, 0))
    if emit_h:
        out_shape = (jax.ShapeDtypeStruct((mp, d), _f32), jax.ShapeDtypeStruct((mp, d), _bf16))
        out_specs = (row, row)
    else:
        out_shape = jax.ShapeDtypeStruct((mp, d), _f32)
        out_specs = row
    return pl.pallas_call(
        functools.partial(_ffn_kernel, n_f=n_f, emit_h=emit_h),
        grid=(mp // tm, n_f),
        in_specs=[row, vec,
                  pl.BlockSpec((d, tf), lambda i, f: (0, f)),
                  pl.BlockSpec((d, tf), lambda i, f: (0, f)),
                  pl.BlockSpec((tf, d), lambda i, f: (f, 0)),
                  vec],
        out_specs=out_specs,
        out_shape=out_shape,
        scratch_shapes=[pltpu.VMEM((tm, d), _bf16), pltpu.VMEM((tm, d), _f32)],
        compiler_params=_cparams(("parallel", "arbitrary")),
        name="ffn_emit_h" if emit_h else "ffn_final",
    )(h, g, w1, w3, w2, g2)


def _glu_kernel(u_ref, wa_ref, wb_ref, o_ref):
    u = u_ref[...]
    o_ref[...] = _dot(u, wa_ref[...]) * _sigmoid(_dot(u, wb_ref[...]))


def _proj_glu(u, wa, wb):
    mp, d = u.shape
    n = wa.shape[1]
    tm, tn = _row_tile(mp, PROJ_ROW_CAP), _col_tile(n)
    return pl.pallas_call(
        _glu_kernel,
        grid=(mp // tm, n // tn),
        in_specs=[pl.BlockSpec((tm, d), lambda i, j: (i, 0)),
                  pl.BlockSpec((d, tn), lambda i, j: (0, j)),
                  pl.BlockSpec((d, tn), lambda i, j: (0, j))],
        out_specs=pl.BlockSpec((tm, tn), lambda i, j: (i, j)),
        out_shape=jax.ShapeDtypeStruct((mp, n), _f32),
        compiler_params=_cparams(("parallel", "parallel")),
        name="proj_glu",
    )(u, wa, wb)


def _proj_kernel(u_ref, w_ref, *refs, mode, scale, out_dtypes):
    if mode == "rope":
        cos_ref, sin_ref = refs[:2]
        outs = refs[2:]
    else:
        outs = refs
    z = _dot(u_ref[...], w_ref[...])
    if mode == "rope":
        cos = cos_ref[...]
        sin = sin_ref[...]
        for j in range(z.shape[1] // LANES):
            cs = slice(j * LANES, (j + 1) * LANES)
            x = z[:, cs]
            r = x * cos + pltpu.roll(x, LANES // 2, 1) * sin
            if scale != 1.0:
                r = r * scale
            for o_ref, dt in zip(outs, out_dtypes):
                o_ref[:, cs] = r.astype(dt)
        return
    if mode == "sigmoid":
        z = _sigmoid(z)
    if scale != 1.0:
        z = z * scale
    for o_ref, dt in zip(outs, out_dtypes):
        o_ref[...] = z.astype(dt)


def _proj(u, w, *, mode="plain", cos=None, sin=None, scale=1.0, out_dtypes=(_f32,), name="proj"):
    mp, d = u.shape
    n = w.shape[1]
    tm, tn = _row_tile(mp, PROJ_ROW_CAP), _col_tile(n)
    in_specs = [pl.BlockSpec((tm, d), lambda i, j: (i, 0)),
                pl.BlockSpec((d, tn), lambda i, j: (0, j))]
    args = [u, w]
    if mode == "rope":
        in_specs += [pl.BlockSpec((tm, LANES), lambda i, j: (i, 0))] * 2
        args += [cos, sin]
    ospec = pl.BlockSpec((tm, tn), lambda i, j: (i, j))
    outs = pl.pallas_call(
        functools.partial(_proj_kernel, mode=mode, scale=scale, out_dtypes=out_dtypes),
        grid=(mp // tm, n // tn),
        in_specs=in_specs,
        out_specs=tuple(ospec for _ in out_dtypes),
        out_shape=tuple(jax.ShapeDtypeStruct((mp, n), dt) for dt in out_dtypes),
        compiler_params=_cparams(("parallel", "parallel")),
        name=name,
    )(*args)
    return outs


def _ln_swish(c, g, b):
    mu = jnp.mean(c, axis=-1, keepdims=True)
    var = jnp.mean(jnp.square(c - mu), axis=-1, keepdims=True)
    y = (c - mu) * lax.rsqrt(var + EPS) * g + b
    return y * _sigmoid(y)


def _conv_prompt_kernel(cur_ref, halo_ref, w_ref, bdw_ref, g_ref, b_ref, o_ref,
                        ext_sc, sh_sc, wb_sc, c_sc, *, width, tc, halo, cblk):
    i = pl.program_id(0)
    ch = cur_ref.shape[1]
    rb = 16
    ext_sc[pl.ds(0, halo), :] = jnp.where(i > 0, halo_ref[...], 0.0)
    ext_sc[pl.ds(halo, tc), :] = cur_ref[...]
    ext_sc[pl.ds(halo + tc, 8), :] = jnp.zeros((8, ch), _f32)
    lead = halo - (width - 1)
    for cb in range(ch // cblk):
        cs = pl.ds(cb * cblk, cblk)
        for r in range(8):
            sh_sc[r] = ext_sc[pl.ds(r, halo + tc), cs]
        for j in range(width):
            wb_sc[j] = jnp.broadcast_to(w_ref[pl.ds(j, 1), cs], (8, cblk))
        bias = jnp.broadcast_to(bdw_ref[:, cs], (8, cblk))

        def rows(t, _):
            t0 = pl.multiple_of(t * rb, rb)
            accs = [bias] * (rb // 8)
            for j in range(width):
                off = lead + j
                wv = wb_sc[j]
                for a in range(rb // 8):
                    x = sh_sc[off % 8, pl.ds(t0 + a * 8 + (off // 8) * 8, 8), :]
                    accs[a] = accs[a] + x * wv
            for a in range(rb // 8):
                c_sc[pl.ds(t0 + a * 8, 8), cs] = accs[a]
            return 0

        lax.fori_loop(0, tc // rb, rows, 0)
    o_ref[...] = _ln_swish(c_sc[...], g_ref[...], b_ref[...]).astype(_bf16)


def _conv_prompt(glu, tp, w_dw, b_dw, ln_g, ln_b):
    ch = glu.shape[1]
    width = w_dw.shape[0]
    tc = Q_TILE
    halo = 32
    assert width - 1 <= halo and tp % tc == 0 and tc % halo == 0
    cblk = min(512, ch)
    vec = pl.BlockSpec((1, ch), lambda i: (0, 0))
    return pl.pallas_call(
        functools.partial(_conv_prompt_kernel, width=width, tc=tc, halo=halo, cblk=cblk),
        grid=(tp // tc,),
        in_specs=[pl.BlockSpec((tc, ch), lambda i: (i, 0)),
                  pl.BlockSpec((halo, ch), lambda i: (jnp.maximum(i * (tc // halo) - 1, 0), 0)),
                  pl.BlockSpec((width, ch), lambda i: (0, 0)),
                  vec, vec, vec],
        out_specs=pl.BlockSpec((tc, ch), lambda i: (i, 0)),
        out_shape=jax.ShapeDtypeStruct((tp, ch), _bf16),
        scratch_shapes=[pltpu.VMEM((halo + tc + 8, ch), _f32),
                        pltpu.VMEM((8, halo + tc, cblk), _f32),
                        pltpu.VMEM((width, 8, cblk), _f32),
                        pltpu.VMEM((tc, ch), _f32)],
        compiler_params=_cparams(("parallel",)),
        name="conv_prompt",
    )(glu, glu, w_dw, b_dw, ln_g, ln_b)


def _conv_sample_kernel(ext_ref, w_ref, bdw_ref, g_ref, b_ref, o_ref, *, width, ds):
    ch = ext_ref.shape[2]
    acc = jnp.broadcast_to(bdw_ref[...], (ds, ch))
    for j in range(width):
        acc = acc + ext_ref[0, pl.ds(j, ds), :] * w_ref[pl.ds(j, 1), :]
    o_ref[0] = _ln_swish(acc, g_ref[...], b_ref[...]).astype(_bf16)


def _conv_sample(ext, w_dw, b_dw, ln_g, ln_b, ds):
    db, rows, ch = ext.shape
    width = w_dw.shape[0]
    vec = pl.BlockSpec((1, ch), lambda b: (0, 0))
    return pl.pallas_call(
        functools.partial(_conv_sample_kernel, width=width, ds=ds),
        grid=(db,),
        in_specs=[pl.BlockSpec((1, rows, ch), lambda b: (b, 0, 0)),
                  pl.BlockSpec((width, ch), lambda b: (0, 0)),
                  vec, vec, vec],
        out_specs=pl.BlockSpec((1, ds, ch), lambda b: (b, 0, 0)),
        out_shape=jax.ShapeDtypeStruct((db, ds, ch), _bf16),
        compiler_params=_cparams(("parallel",)),
        name="conv_sample",
    )(ext, w_dw, b_dw, ln_g, ln_b)


def _merge_kernel(ca_ref, ao_ref, wco_ref, wo_ref, bco_ref, sgc_ref, sga_ref, m_ref):
    yc = _dot(ca_ref[...], wco_ref[...]) + bco_ref[...]
    ya = _dot(ao_ref[...], wo_ref[...])
    m_ref[...] = (sgc_ref[...].astype(_f32) * yc + sga_ref[...].astype(_f32) * ya).astype(_bf16)


def _merge(ca, ao, wco, wo, bco, sg):
    mp, ch = ca.shape
    aw = ao.shape[1]
    d = wco.shape[1]
    tm = _row_tile(mp, PROJ_ROW_CAP)
    tn = min(512, d)
    nj = d // tn
    return pl.pallas_call(
        _merge_kernel,
        grid=(mp // tm, nj),
        in_specs=[pl.BlockSpec((tm, ch), lambda i, j: (i, 0)),
                  pl.BlockSpec((tm, aw), lambda i, j: (i, 0)),
                  pl.BlockSpec((ch, tn), lambda i, j: (0, j)),
                  pl.BlockSpec((aw, tn), lambda i, j: (0, j)),
                  pl.BlockSpec((1, tn), lambda i, j: (0, j)),
                  pl.BlockSpec((tm, tn), lambda i, j: (i, j)),
                  pl.BlockSpec((tm, tn), lambda i, j: (i, j + nj))],
        out_specs=pl.BlockSpec((tm, tn), lambda i, j: (i, j)),
        out_shape=jax.ShapeDtypeStruct((mp, d), _bf16),
        compiler_params=_cparams(("parallel", "parallel")),
        name="merge",
    )(ca, ao, wco, wo, bco, sg, sg)


def _resid_kernel(m_ref, w_ref, h_ref, o_ref):
    o_ref[...] = h_ref[...] + _dot(m_ref[...], w_ref[...])


def _resid_proj(m, w, h):
    mp, d = m.shape
    n = w.shape[1]
    tm, tn = _row_tile(mp, PROJ_ROW_CAP), _col_tile(n)
    return pl.pallas_call(
        _resid_kernel,
        grid=(mp // tm, n // tn),
        in_specs=[pl.BlockSpec((tm, d), lambda i, j: (i, 0)),
                  pl.BlockSpec((d, tn), lambda i, j: (0, j)),
                  pl.BlockSpec((tm, tn), lambda i, j: (i, j))],
        out_specs=pl.BlockSpec((tm, tn), lambda i, j: (i, j)),
        out_shape=jax.ShapeDtypeStruct((mp, n), _f32),
        compiler_params=_cparams(("parallel", "parallel")),
        name="resid_proj",
    )(m, w, h)


def _order_key(s):
    b = lax.bitcast_convert_type(s, jnp.int32)
    return b ^ (lax.shift_right_arithmetic(b, 31) & jnp.int32(0x7FFFFFFF))


def _bit_search(count_ge, rows, n_bits, k, lowest):
    lowest = jnp.int32(lowest)

    def body(b, t_u):
        bit = lax.shift_left(jnp.int32(1), jnp.int32(n_bits - 1) - jnp.asarray(b, jnp.int32))
        cand = t_u | bit
        cnt = count_ge(cand + lowest)
        return jnp.where(cnt >= k, cand, t_u)

    t_u = lax.fori_loop(0, n_bits, body, jnp.zeros((rows, 1), jnp.int32))
    return t_u + lowest


def _kth_key_packed(load, pk_ref, scan, rows, width, k):
    halves = width // LANES
    i16 = jnp.int16
    half_min, half_max = -(2 ** 15), 2 ** 15 - 1

    def count16(th):
        th16 = th.astype(i16)

        def one(c, acc):
            w = jnp.where(pk_ref[c] >= th16, i16(1), i16(0))
            for j in range(halves):
                acc = acc + w[:, j * LANES:(j + 1) * LANES]
            return acc
        acc = scan(one, jnp.zeros((rows, LANES), i16))
        return jnp.sum(acc.astype(_f32), axis=1, keepdims=True)

    def fill_top(c, carry):
        pk_ref[c] = lax.shift_right_arithmetic(load(c), 16).astype(i16)
        return carry
    scan(fill_top, 0)
    top = _bit_search(count16, rows, 16, k, half_min)
    above = jnp.where(top >= half_max, 0.0, count16(jnp.minimum(top + 1, half_max)))

    def fill_low(c, carry):
        x = load(c)
        low = (x & jnp.int32(0xFFFF)) + half_min
        pk_ref[c] = jnp.where(lax.shift_right_arithmetic(x, 16) == top, low, half_min).astype(i16)
        return carry
    scan(fill_low, 0)
    low = _bit_search(count16, rows, 16, k - above, half_min)
    return lax.shift_left(top, 16) | (low - half_min)


def _select_threshold(load, store, n_iters, unroll, rows, width, k, idx_bits, pk_ref=None):
    halves = width // LANES

    def scan(fn, init):
        def body(it, carry):
            for u in range(unroll):
                carry = fn(it * unroll + u, carry)
            return carry
        return lax.fori_loop(0, n_iters, body, init)

    def count(pred):
        def one(c, acc):
            m = pred(load(c), c)
            for j in range(halves):
                acc = acc + jnp.where(m[:, j * LANES:(j + 1) * LANES], 1.0, 0.0)
            return acc
        return jnp.sum(scan(one, jnp.zeros((rows, LANES), _f32)), axis=1, keepdims=True)

    kf = jnp.float32(k)
    if pk_ref is None:
        t = _bit_search(lambda th: count(lambda x, c: x >= th), rows, 32, kf, INT_MIN)
    else:
        t = _kth_key_packed(load, pk_ref, scan, rows, width, kf)
    t = jnp.maximum(t, jnp.int32(INT_MIN + 1))
    n_ge = count(lambda x, c: x >= t)

    @pl.when(jnp.max(n_ge) > kf)
    def _():
        n_gt = count(lambda x, c: x > t)
        need = kf - n_gt
        lane = lax.broadcasted_iota(jnp.int32, (rows, width), 1)

        def ties_below(j):
            return count(lambda x, c: (x == t) & ((lane + c * width) < j))

        j_cut = _bit_search(lambda j: need - ties_below(j) + (kf - 1.0), rows, idx_bits, kf, 0)
        surplus = n_ge > kf

        def rewrite(c, carry):
            x = load(c)
            drop = (x == t) & ((lane + c * width) > j_cut) & surplus
            store(c, jnp.where(drop, jnp.int32(INT_MIN), x))
            return carry
        scan(rewrite, 0)

    return t


def _prompt_attn_kernel(iq_ref, iw_ref, q_ref, ikt_ref, kt_ref, v_ref, o_ref,
                        key_sc, pk_sc, thr_sc, wb_sc, m_sc, acc_sc,
                        *, nih, n_heads, n_kv, topk, idx_bits, sub):
    i = pl.program_id(0)
    tq = q_ref.shape[0]
    group = n_heads // n_kv
    unroll = 2

    for h in range(nih):
        wb_sc[h] = jnp.broadcast_to(iw_ref[:, h:h + 1], (tq, LANES))

    def scores(c):
        rhs = ikt_ref[c]
        halves = [jnp.zeros((tq, LANES), _f32) for _ in range(tq // LANES)]
        for h in range(nih):
            d = _dot(iq_ref[:, h * LANES:(h + 1) * LANES], rhs)
            w = wb_sc[h]
            for j in range(len(halves)):
                halves[j] = halves[j] + jnp.maximum(d[:, j * LANES:(j + 1) * LANES], 0.0) * w
        return _order_key(jnp.concatenate(halves, axis=1))

    def score_body(c, _):
        key_sc[c] = scores(c)
        return 0

    lax.fori_loop(0, i, score_body, 0)
    row = lax.broadcasted_iota(jnp.int32, (tq, tq), 0)
    col = lax.broadcasted_iota(jnp.int32, (tq, tq), 1)
    key_sc[i] = jnp.where(col <= row, scores(i), jnp.int32(INT_MIN))
    key_sc[i + 1] = jnp.full((tq, tq), INT_MIN, jnp.int32)
    n_iters = lax.shift_right_logical(i + unroll, 1)

    for rb in range(tq // sub):
        rs = pl.ds(rb * sub, sub)
        t = _select_threshold(lambda c: key_sc[c, rs, :],
                              lambda c, x: key_sc.__setitem__((c, rs, slice(None)), x),
                              n_iters, unroll, sub, tq, topk, idx_bits, pk_ref=pk_sc)
        thr_sc[rs, :] = jnp.broadcast_to(t, (sub, LANES))

    thr = jnp.concatenate([thr_sc[...]] * (tq // LANES), axis=1)
    m_sc[...] = jnp.full(m_sc.shape, NEG, _f32)
    acc_sc[...] = jnp.zeros_like(acc_sc)
    ones = jnp.ones((tq, LANES), _bf16)

    def attend(c, _):
        bias = jnp.where(key_sc[c] >= thr, 0.0, NEG)
        vc = v_ref[c]

        for n in range(n_kv):
            kt = kt_ref[c, n]
            vn1 = jnp.concatenate([vc[:, n * LANES:(n + 1) * LANES], ones], axis=1)
            for g in range(group):
                h = n * group + g
                lg = _dot(q_ref[:, h * LANES:(h + 1) * LANES], kt) + bias
                m_prev = m_sc[h]
                m_new = jnp.maximum(m_prev, jnp.max(lg, axis=1, keepdims=True))
                alpha = jnp.exp2(m_prev - m_new)
                p = jnp.exp2(lg - jnp.concatenate([m_new] * (tq // LANES), axis=1))
                acc_sc[h] = jnp.concatenate([alpha, alpha], axis=1) * acc_sc[h] + _dot(p.astype(_bf16), vn1)
                m_sc[h] = m_new
        return 0

    lax.fori_loop(0, i + 1, attend, 0)
    for h in range(n_heads):
        a = acc_sc[h]
        o_ref[:, h * LANES:(h + 1) * LANES] = (a[:, :LANES] / a[:, LANES:]).astype(_bf16)


def _prompt_attention(iq, iw, q, ikt, kt, v, *, tp, nih, n_heads, n_kv, topk):
    tq = Q_TILE
    nc = tp // tq
    resident = lambda shape: pl.BlockSpec(shape, lambda i: (0,) * len(shape),
                                          pipeline_mode=pl.Buffered(1))
    idx_bits = max(1, int(np.ceil(np.log2(tp + 1))))
    sub = min(SEARCH_ROWS, tq)
    return pl.pallas_call(
        functools.partial(_prompt_attn_kernel, nih=nih, n_heads=n_heads, n_kv=n_kv, topk=topk,
                          idx_bits=idx_bits, sub=sub),
        grid=(nc,),
        in_specs=[pl.BlockSpec((tq, nih * LANES), lambda i: (i, 0)),
                  pl.BlockSpec((tq, nih), lambda i: (i, 0)),
                  pl.BlockSpec((tq, n_heads * LANES), lambda i: (i, 0)),
                  resident((nc, LANES, tq)),
                  resident((nc, n_kv, LANES, tq)),
                  resident((nc, tq, n_kv * LANES))],
        out_specs=pl.BlockSpec((tq, n_heads * LANES), lambda i: (i, 0)),
        out_shape=jax.ShapeDtypeStruct((tp, n_heads * LANES), _bf16),
        scratch_shapes=[pltpu.VMEM((nc + 1, tq, tq), jnp.int32),
                        pltpu.VMEM((nc + 1, sub, tq), jnp.int16),
                        pltpu.VMEM((tq, LANES), jnp.int32),
                        pltpu.VMEM((nih, tq, LANES), _f32),
                        pltpu.VMEM((n_heads, tq, LANES), _f32),
                        pltpu.VMEM((n_heads, tq, 2 * LANES), _f32)],
        compiler_params=_cparams(("arbitrary",)),
        name="prompt_attention",
    )(iq, iw, q, ikt, kt, v)


def _ring_slot(g):
    return g % RING_SLOTS if isinstance(g, int) else lax.rem(g, RING_SLOTS)


def _ring_copies(pt_ref, streams, sem_ref, pps, g):
    slot = _ring_slot(g)
    return [pltpu.make_async_copy(cache.at[pt_ref[g * pps + j]], buf.at[slot, j], sem_ref.at[slot])
            for j in range(pps) for cache, buf in streams]


def _ring_step(pt_ref, streams, sem_ref, pps, g, n_total):
    ahead = RING_SLOTS - 1

    def request(step):
        for n, cp in enumerate(_ring_copies(pt_ref, streams, sem_ref, pps, step)):
            cp.start(priority=n % 2)

    @pl.when(g == 0)
    def _():
        for step in range(min(ahead, n_total)):
            request(step)

    @pl.when(g + ahead < n_total)
    def _():
        request(g + ahead)

    for cp in _ring_copies(pt_ref, streams, sem_ref, pps, g):
        cp.wait()


def _sample_index_kernel(pt_ref, iq_ref, iw_ref, ikn_ref, cache_ref, key_ref, thr_ref, buf, sem,
                         *, pps, n_steps, n_total, n_pages, n_chunks, ds, nih, topk, idx_bits):
    s = pl.program_id(1)
    g = pl.program_id(0) * n_steps + s
    _ring_step(pt_ref, [(cache_ref, buf)], sem, pps, g, n_total)
    slot = _ring_slot(g)

    def index_keys(ik_rows):
        n_keys = ik_rows.shape[0]
        w = jnp.concatenate([iw_ref[0]] * (n_keys // LANES), axis=1)
        x = jnp.maximum(_dot_nt(iq_ref[0], ik_rows), 0.0) * w
        return _order_key(jnp.sum(x.reshape(nih, ds, n_keys), axis=0))

    pair = 2 if pps % 2 == 0 else 1
    for j in range(0, pps, pair):
        pages = jnp.concatenate([buf[slot, j + a].astype(_bf16) for a in range(pair)], axis=0)
        keys = index_keys(pages)
        for a in range(pair):
            key_ref[0, s * pps + j + a] = keys[:, a * LANES:(a + 1) * LANES]

    @pl.when(s == n_steps - 1)
    def _():
        qi = lax.broadcasted_iota(jnp.int32, (ds, LANES), 0)
        ki = lax.broadcasted_iota(jnp.int32, (ds, LANES), 1)
        key_ref[0, n_pages] = jnp.where(ki <= qi, index_keys(ikn_ref[0]), jnp.int32(INT_MIN))
        for c in range(n_pages + 1, n_chunks):
            key_ref[0, c] = jnp.full((ds, LANES), INT_MIN, jnp.int32)
        t = _select_threshold(lambda c: key_ref[0, c],
                              lambda c, x: key_ref.__setitem__((0, c), x),
                              n_chunks // SEARCH_UNROLL_SAMPLE, SEARCH_UNROLL_SAMPLE,
                              ds, LANES, topk, idx_bits)
        thr_ref[0] = jnp.broadcast_to(t, (ds, LANES))


def _sample_index(page_table, iq_r, iw_r, ik_new, cache_ik, *, ds, nih, topk):
    db, n_pages = page_table.shape
    pps = min(PAGES_PER_STEP_INDEX, n_pages)
    assert n_pages % pps == 0
    n_steps = n_pages // pps
    page = cache_ik.shape[1]
    n_chunks = _round_up(n_pages + 1, SEARCH_UNROLL_SAMPLE)
    rows_i = nih * ds
    idx_bits = max(1, int(np.ceil(np.log2(n_chunks * LANES + 1))))
    per_b = lambda shape: pl.BlockSpec((1,) + shape, lambda b, s, pt: (b,) + (0,) * len(shape))

    grid_spec = pltpu.PrefetchScalarGridSpec(
        num_scalar_prefetch=1,
        grid=(db, n_steps),
        in_specs=[per_b((rows_i, LANES)), per_b((rows_i, LANES)), per_b((LANES, LANES)),
                  pl.BlockSpec(memory_space=pl.ANY)],
        out_specs=(per_b((n_chunks, ds, LANES)), per_b((ds, LANES))),
        scratch_shapes=[pltpu.VMEM((RING_SLOTS, pps, page, LANES), _f32),
                        pltpu.SemaphoreType.DMA((RING_SLOTS,))],
    )
    return pl.pallas_call(
        functools.partial(_sample_index_kernel, pps=pps, n_steps=n_steps, n_total=db * n_steps, n_pages=n_pages,
                          n_chunks=n_chunks, ds=ds, nih=nih, topk=topk, idx_bits=idx_bits),
        grid_spec=grid_spec,
        out_shape=(jax.ShapeDtypeStruct((db, n_chunks, ds, LANES), jnp.int32),
                   jax.ShapeDtypeStruct((db, ds, LANES), jnp.int32)),
        compiler_params=_cparams(("arbitrary", "arbitrary")),
        name="sample_index",
    )(page_table.reshape(-1), iq_r, iw_r, ik_new, cache_ik)


def _sample_attend_kernel(pt_ref, q_ref, key_ref, thr_ref, kn_ref, vn_ref, ck_ref, cv_ref, o_ref,
                          kbuf, vbuf, sem, m_sc, l_sc, acc_sc,
                          *, pps, n_steps, n_total, n_pages, ds, n_heads, n_kv):
    s = pl.program_id(1)
    g = pl.program_id(0) * n_steps + s
    _ring_step(pt_ref, [(ck_ref, kbuf), (cv_ref, vbuf)], sem, pps, g, n_total)
    slot = _ring_slot(g)
    group = n_heads // n_kv
    rows_g = group * ds
    page = kbuf.shape[2] // n_kv

    @pl.when(s == 0)
    def _():
        m_sc[...] = jnp.full(m_sc.shape, NEG, _f32)
        l_sc[...] = jnp.zeros_like(l_sc)
        acc_sc[...] = jnp.zeros_like(acc_sc)

    def attend(chunk_ids, k_of, v_of):
        thr = thr_ref[0]
        bias = jnp.concatenate([jnp.where(key_ref[0, c] >= thr, 0.0, NEG) for c in chunk_ids], axis=1)
        bias = jnp.concatenate([bias] * group, axis=0)
        m_prev, l_prev, acc_prev = m_sc[...], l_sc[...], acc_sc[...]
        heads = [slice(n * rows_g, (n + 1) * rows_g) for n in range(n_kv)]
        lgs = [_dot_nt(q_ref[0, rs, :],
                       jnp.concatenate([k_of(j, n) for j in range(len(chunk_ids))], axis=0)) + bias
               for n, rs in enumerate(heads)]
        m_out = [jnp.maximum(m_prev[rs], jnp.max(lg, axis=1, keepdims=True)) for lg, rs in zip(lgs, heads)]
        ps = [jnp.exp2(lg - m_new[:, :1]) for lg, m_new in zip(lgs, m_out)]
        pvs = [_dot(p.astype(_bf16), jnp.concatenate([v_of(j, n) for j in range(len(chunk_ids))], axis=0))
               for n, p in enumerate(ps)]
        l_out, acc_out = [], []
        for rs, m_new, p, pv in zip(heads, m_out, ps, pvs):
            alpha = jnp.exp2(m_prev[rs] - m_new)
            l_out.append(alpha * l_prev[rs] + jnp.sum(p, axis=1, keepdims=True))
            acc_out.append(alpha * acc_prev[rs] + pv)
        m_sc[...] = jnp.concatenate(m_out, axis=0)
        l_sc[...] = jnp.concatenate(l_out, axis=0)
        acc_sc[...] = jnp.concatenate(acc_out, axis=0)

    def head_rows(buf, j, n):
        return buf[slot, j, pl.ds(n, page, stride=n_kv), :].astype(_bf16)

    attend([s * pps + j for j in range(pps)],
           lambda j, n: head_rows(kbuf, j, n), lambda j, n: head_rows(vbuf, j, n))

    @pl.when(s == n_steps - 1)
    def _():
        attend([n_pages],
               lambda j, n: kn_ref[0, :, n * LANES:(n + 1) * LANES],
               lambda j, n: vn_ref[0, :, n * LANES:(n + 1) * LANES])
        o_ref[0] = (acc_sc[...] / l_sc[...]).astype(_bf16)


def _sample_attend(page_table, q_r, keys, thr, k_new, v_new, cache_k, cache_v, *, ds, n_heads, n_kv):
    db, n_pages = page_table.shape
    pps = min(PAGES_PER_STEP_ATTEND, n_pages)
    assert n_pages % pps == 0
    n_steps = n_pages // pps
    n_chunks = keys.shape[1]
    rows_q = n_heads * ds
    kvw = n_kv * LANES
    page_rows = cache_k.shape[1]
    per_b = lambda shape: pl.BlockSpec((1,) + shape, lambda b, s, pt: (b,) + (0,) * len(shape))
    hbm = pl.BlockSpec(memory_space=pl.ANY)

    grid_spec = pltpu.PrefetchScalarGridSpec(
        num_scalar_prefetch=1,
        grid=(db, n_steps),
        in_specs=[per_b((rows_q, LANES)), per_b((n_chunks, ds, LANES)), per_b((ds, LANES)),
                  per_b((LANES, kvw)), per_b((LANES, kvw)), hbm, hbm],
        out_specs=per_b((rows_q, LANES)),
        scratch_shapes=[pltpu.VMEM((RING_SLOTS, pps, page_rows, LANES), _f32),
                        pltpu.VMEM((RING_SLOTS, pps, page_rows, LANES), _f32),
                        pltpu.SemaphoreType.DMA((RING_SLOTS,)),
                        pltpu.VMEM((rows_q, LANES), _f32),
                        pltpu.VMEM((rows_q, LANES), _f32),
                        pltpu.VMEM((rows_q, LANES), _f32)],
    )
    return pl.pallas_call(
        functools.partial(_sample_attend_kernel, pps=pps, n_steps=n_steps, n_total=db * n_steps,
                          n_pages=n_pages, ds=ds,
                          n_heads=n_heads, n_kv=n_kv),
        grid_spec=grid_spec,
        out_shape=jax.ShapeDtypeStruct((db, rows_q, LANES), _bf16),
        compiler_params=_cparams(("arbitrary", "arbitrary")),
        name="sample_attend",
    )(page_table.reshape(-1), q_r, keys, thr, k_new, v_new, cache_k, cache_v)


def _rope_tables(pos, dim):
    half = dim // 2
    inv = ROPE_THETA ** (-jnp.arange(half, dtype=_f32) / half)
    ang = pos.astype(_f32)[:, None] * inv[None, :]
    cos, sin = jnp.cos(ang), jnp.sin(ang)
    return jnp.concatenate([cos, cos], axis=1), jnp.concatenate([-sin, sin], axis=1)


def kernel(x_prompt, x_sample, cache_k, cache_v, cache_ik, state_conv, page_table, meta,
           norm1_g, ffn1_w1, ffn1_w3, ffn1_w2, norm_mix_g, w_in, w_dw, b_dw, conv_ln_g, conv_ln_b,
           w_conv_out, b_conv_out, w_o, w_out, norm2_g, ffn2_w1, ffn2_w3, ffn2_w2, final_g):
    bsz, seq, d = x_prompt.shape
    db, ds, _ = x_sample.shape
    depth, n_pool, page, n_kv, hd = cache_k.shape
    idx_dim = cache_ik.shape[-1]
    n_meta = meta.shape[0]
    width = w_dw.shape[1]
    ch = w_dw.shape[2]
    attn_w = w_o.shape[1]
    n_heads = attn_w // hd
    kv_w = n_kv * hd
    in_w = w_in.shape[2]
    nih = (in_w - 2 * ch - attn_w - 2 * kv_w - idx_dim - 2 * d) // (idx_dim + 1)
    n_pages = page_table.shape[1]
    past = n_pages * page
    assert bsz == 1 and depth == 1 and hd == LANES and idx_dim == LANES and page == LANES
    assert 2 * ch + attn_w + 2 * kv_w + nih * idx_dim + idx_dim + nih + 2 * d == in_w
    assert (n_heads * ds) % 8 == 0 and ds % 8 == 0 and ds <= LANES

    t = n_meta + seq
    tp = _round_up(t, Q_TILE)
    ms = db * ds
    mp = _round_up(tp + ms, FFN_ROW_TILE)
    topk_p = min(MAX_TOPK, seq // 4)
    topk_s = min(MAX_TOPK, (past + ds) // 4)
    assert topk_p <= Q_TILE and topk_s <= LANES * (n_pages + 1)

    h0 = jnp.concatenate([meta, x_prompt[0], jnp.zeros((tp - t, d), _f32),
                          x_sample.reshape(ms, d), jnp.zeros((mp - tp - ms, d), _f32)], axis=0)
    pos = jnp.concatenate([jnp.arange(t, dtype=jnp.int32), jnp.zeros((tp - t,), jnp.int32),
                           jnp.tile(past + jnp.arange(ds, dtype=jnp.int32), db),
                           jnp.zeros((mp - tp - ms,), jnp.int32)])
    cos, sin = _rope_tables(pos, hd)

    vec = lambda a: a.reshape(1, -1).astype(_f32)
    cast = lambda a: a.astype(_bf16)
    offs = np.cumsum([0, ch, ch, attn_w, kv_w, kv_w, nih * idx_dim, idx_dim, nih, d, d])
    wi = w_in[0]
    cols = lambda a, b: cast(wi[:, offs[a]:offs[b]])

    h1, u = _ffn(h0, vec(norm1_g[0]), cast(ffn1_w1[0]), cast(ffn1_w3[0]), cast(ffn1_w2[0]),
                 vec(norm_mix_g[0]), emit_h=True)

    glu = _proj_glu(u, cols(0, 1), cols(1, 2))
    (q,) = _proj(u, cols(2, 3), mode="rope", cos=cos, sin=sin, scale=hd ** -0.5 * LOG2E,
                 out_dtypes=(_bf16,), name="proj_q")
    k32, k16 = _proj(u, cols(3, 4), mode="rope", cos=cos, sin=sin, out_dtypes=(_f32, _bf16), name="proj_k")
    v32, v16 = _proj(u, cols(4, 5), out_dtypes=(_f32, _bf16), name="proj_v")
    (iq,) = _proj(u, cols(5, 6), mode="rope", cos=cos, sin=sin, out_dtypes=(_bf16,), name="proj_iq")
    ik32, ik16 = _proj(u, cols(6, 7), mode="rope", cos=cos, sin=sin, out_dtypes=(_f32, _bf16), name="proj_ik")
    (iw,) = _proj(u, cols(7, 8), scale=(idx_dim ** -0.5) * (nih ** -0.5), name="proj_iw")
    (sg,) = _proj(u, cols(8, 10), mode="sigmoid", out_dtypes=(_bf16,), name="proj_gates")

    ca_p = _conv_prompt(glu, tp, w_dw[0], vec(b_dw[0]), vec(conv_ln_g[0]), vec(conv_ln_b[0]))
    glu_s = glu[tp:tp + ms].reshape(db, ds, ch)
    ext_rows = _round_up(width - 1 + ds, 8)
    ext_s = jnp.concatenate([state_conv[0], glu_s,
                             jnp.zeros((db, ext_rows - (width - 1 + ds), ch), _f32)], axis=1)
    ca_s = _conv_sample(ext_s, w_dw[0], vec(b_dw[0]), vec(conv_ln_g[0]), vec(conv_ln_b[0]), ds)
    ca = jnp.concatenate([ca_p, ca_s.reshape(ms, ch), jnp.zeros((mp - tp - ms, ch), _bf16)], axis=0)

    nc = tp // Q_TILE
    ikt = ik16[:tp].reshape(nc, Q_TILE, idx_dim).transpose(0, 2, 1)
    kt = k16[:tp].reshape(nc, Q_TILE, n_kv, hd).transpose(0, 2, 3, 1)
    vch = v16[:tp].reshape(nc, Q_TILE, kv_w)
    ao_p = _prompt_attention(iq, iw, q, ikt, kt, vch, tp=tp, nih=nih, n_heads=n_heads, n_kv=n_kv,
                             topk=topk_p)

    sl = slice(tp, tp + ms)
    iq_r = iq[sl].reshape(db, ds, nih, idx_dim).transpose(0, 2, 1, 3).reshape(db, nih * ds, idx_dim)
    iw_r = jnp.broadcast_to(iw[sl].reshape(db, ds, nih).transpose(0, 2, 1).reshape(db, nih * ds, 1),
                            (db, nih * ds, LANES))
    q_r = q[sl].reshape(db, ds, n_heads, hd).transpose(0, 2, 1, 3).reshape(db, n_heads * ds, hd)
    pad_keys = lambda a: jnp.pad(a.reshape(db, ds, -1), ((0, 0), (0, LANES - ds), (0, 0)))
    keys_s, thr_s = _sample_index(page_table, iq_r, iw_r, pad_keys(ik16[sl]), cache_ik[0],
                                  ds=ds, nih=nih, topk=topk_s)
    ao_s = _sample_attend(page_table, q_r, keys_s, thr_s, pad_keys(k16[sl]), pad_keys(v16[sl]),
                          cache_k[0].reshape(n_pool, page * n_kv, hd),
                          cache_v[0].reshape(n_pool, page * n_kv, hd),
                          ds=ds, n_heads=n_heads, n_kv=n_kv)
    ao_s = ao_s.reshape(db, n_heads, ds, hd).transpose(0, 2, 1, 3).reshape(ms, attn_w)
    ao = jnp.concatenate([ao_p, ao_s, jnp.zeros((mp - tp - ms, attn_w), _bf16)], axis=0)

    m = _merge(ca, ao, cast(w_conv_out[0]), cast(w_o[0]), vec(b_conv_out[0]), sg)
    h2 = _resid_proj(m, cast(w_out[0]), h1)
    y = _ffn(h2, vec(norm2_g[0]), cast(ffn2_w1[0]), cast(ffn2_w3[0]), cast(ffn2_w2[0]),
             vec(final_g), emit_h=False)

    y_prompt = y[n_meta:t][None]
    y_sample = y[sl].reshape(db, ds, d)
    hist = width - 1
    new_conv_p = glu[t - hist:t][None, None]
    new_conv_s = jnp.concatenate([state_conv[0], glu_s], axis=1)[:, -hist:][None]
    return (y_prompt, y_sample,
            k32[:t].reshape(1, 1, t, n_kv, hd), v32[:t].reshape(1, 1, t, n_kv, hd),
            ik32[:t].reshape(1, 1, t, idx_dim), new_conv_p,
            k32[sl].reshape(1, db, ds, n_kv, hd), v32[sl].reshape(1, db, ds, n_kv, hd),
            ik32[sl].reshape(1, db, ds, idx_dim), new_conv_s)
```

```python
import functools

import numpy as np
import jax
import jax.numpy as jnp
from jax import lax
from jax.experimental import pallas as pl
from jax.experimental.pallas import tpu as pltpu

MAX_TOPK = 256
ROPE_THETA = 10000.0
EPS = 1e-6

LANES = 128
FFN_ROW_TILE = 512
FF_TILE = 512
PROJ_ROW_CAP = 1088
PROJ_COL_TILE = 1024
Q_TILE = 256
PAGES_PER_STEP_INDEX = 64
PAGES_PER_STEP_ATTEND = 16
RING_SLOTS = 4
SEARCH_UNROLL_SAMPLE = 16
SEARCH_ROWS = 128
VMEM_LIMIT = 56 * 1024 * 1024
INT_MIN = -(2 ** 31)
NEG = -1e30
LOG2E = 1.4426950408889634

_bf16 = jnp.bfloat16
_f32 = jnp.float32


def _cparams(sem):
    return pltpu.CompilerParams(dimension_semantics=sem, vmem_limit_bytes=VMEM_LIMIT)


def _dot(a, b):
    return jnp.dot(a, b, preferred_element_type=_f32)


def _dot_nt(a, b):
    return lax.dot_general(a, b, (((1,), (1,)), ((), ())), preferred_element_type=_f32)


def _rms(x, g):
    return x * lax.rsqrt(jnp.mean(x * x, axis=-1, keepdims=True) + EPS) * g


def _sigmoid(x):
    return 1.0 / (1.0 + jnp.exp(-x))


def _round_up(x, m):
    return (x + m - 1) // m * m


def _row_tile(rows, cap):
    return max(t for t in range(16, cap + 1, 16) if rows % t == 0)


def _col_tile(n):
    return PROJ_COL_TILE if n % PROJ_COL_TILE == 0 else (512 if n % 512 == 0 else n)


def _ffn_kernel(h_ref, g_ref, w1_ref, w3_ref, w2_ref, g2_ref, *refs, n_f, emit_h):
    if emit_h:
        h_out_ref, u_out_ref, u_sc, acc_sc = refs
    else:
        y_out_ref, u_sc, acc_sc = refs
    f = pl.program_id(1)

    @pl.when(f == 0)
    def _():
        u_sc[...] = _rms(h_ref[...], g_ref[...]).astype(_bf16)
        acc_sc[...] = jnp.zeros_like(acc_sc)

    u = u_sc[...]
    a = _dot(u, w1_ref[...])
    b = _dot(u, w3_ref[...])
    gate = (a * _sigmoid(a)) * b
    acc_sc[...] += _dot(gate.astype(_bf16), w2_ref[...])

    @pl.when(f == n_f - 1)
    def _():
        hn = h_ref[...] + 0.5 * acc_sc[...]
        un = _rms(hn, g2_ref[...])
        if emit_h:
            h_out_ref[...] = hn
            u_out_ref[...] = un.astype(_bf16)
        else:
            y_out_ref[...] = un


def _ffn(h, g, w1, w3, w2, g2, *, emit_h):
    mp, d = h.shape
    dff = w1.shape[1]
    tm = FFN_ROW_TILE
    tf = min(FF_TILE, dff)
    assert mp % tm == 0 and dff % tf == 0
    n_f = dff // tf
    row = pl.BlockSpec((tm, d), lambda i, f: (i, 0))
    vec = pl.BlockSpec((1, d), lambda i, f: (0, 0))
    if emit_h:
        out_shape = (jax.ShapeDtypeStruct((mp, d), _f32), jax.ShapeDtypeStruct((mp, d), _bf16))
        out_specs = (row, row)
    else:
        out_shape = jax.ShapeDtypeStruct((mp, d), _f32)
        out_specs = row
    return pl.pallas_call(
        functools.partial(_ffn_kernel, n_f=n_f, emit_h=emit_h),
        grid=(mp // tm, n_f),
        in_specs=[row, vec,
                  pl.BlockSpec((d, tf), lambda i, f: (0, f)),
                  pl.BlockSpec((d, tf), lambda i, f: (0, f)),
                  pl.BlockSpec((tf, d), lambda i, f: (f, 0)),
                  vec],
        out_specs=out_specs,
        out_shape=out_shape,
        scratch_shapes=[pltpu.VMEM((tm, d), _bf16), pltpu.VMEM((tm, d), _f32)],
        compiler_params=_cparams(("parallel", "arbitrary")),
        name="ffn_emit_h" if emit_h else "ffn_final",
    )(h, g, w1, w3, w2, g2)


def _glu_kernel(u_ref, wa_ref, wb_ref, o_ref):
    u = u_ref[...]
    o_ref[...] = _dot(u, wa_ref[...]) * _sigmoid(_dot(u, wb_ref[...]))


def _proj_glu(u, wa, wb):
    mp, d = u.shape
    n = wa.shape[1]
    tm, tn = _row_tile(mp, PROJ_ROW_CAP), _col_tile(n)
    return pl.pallas_call(
        _glu_kernel,
        grid=(mp // tm, n // tn),
        in_specs=[pl.BlockSpec((tm, d), lambda i, j: (i, 0)),
                  pl.BlockSpec((d, tn), lambda i, j: (0, j)),
                  pl.BlockSpec((d, tn), lambda i, j: (0, j))],
        out_specs=pl.BlockSpec((tm, tn), lambda i, j: (i, j)),
        out_shape=jax.ShapeDtypeStruct((mp, n), _f32),
        compiler_params=_cparams(("parallel", "parallel")),
        name="proj_glu",
    )(u, wa, wb)


def _proj_kernel(u_ref, w_ref, *refs, mode, scale, out_dtypes):
    if mode == "rope":
        cos_ref, sin_ref = refs[:2]
        outs = refs[2:]
    else:
        outs = refs
    z = _dot(u_ref[...], w_ref[...])
    if mode == "rope":
        cos = cos_ref[...]
        sin = sin_ref[...]
        for j in range(z.shape[1] // LANES):
            cs = slice(j * LANES, (j + 1) * LANES)
            x = z[:, cs]
            r = x * cos + pltpu.roll(x, LANES // 2, 1) * sin
            if scale != 1.0:
                r = r * scale
            for o_ref, dt in zip(outs, out_dtypes):
                o_ref[:, cs] = r.astype(dt)
        return
    if mode == "sigmoid":
        z = _sigmoid(z)
    if scale != 1.0:
        z = z * scale
    for o_ref, dt in zip(outs, out_dtypes):
        o_ref[...] = z.astype(dt)


def _proj(u, w, *, mode="plain", cos=None, sin=None, scale=1.0, out_dtypes=(_f32,), name="proj"):
    mp, d = u.shape
    n = w.shape[1]
    tm, tn = _row_tile(mp, PROJ_ROW_CAP), _col_tile(n)
    in_specs = [pl.BlockSpec((tm, d), lambda i, j: (i, 0)),
                pl.BlockSpec((d, tn), lambda i, j: (0, j))]
    args = [u, w]
    if mode == "rope":
        in_specs += [pl.BlockSpec((tm, LANES), lambda i, j: (i, 0))] * 2
        args += [cos, sin]
    ospec = pl.BlockSpec((tm, tn), lambda i, j: (i, j))
    outs = pl.pallas_call(
        functools.partial(_proj_kernel, mode=mode, scale=scale, out_dtypes=out_dtypes),
        grid=(mp // tm, n // tn),
        in_specs=in_specs,
        out_specs=tuple(ospec for _ in out_dtypes),
        out_shape=tuple(jax.ShapeDtypeStruct((mp, n), dt) for dt in out_dtypes),
        compiler_params=_cparams(("parallel", "parallel")),
        name=name,
    )(*args)
    return outs


def _ln_swish(c, g, b):
    mu = jnp.mean(c, axis=-1, keepdims=True)
    var = jnp.mean(jnp.square(c - mu), axis=-1, keepdims=True)
    y = (c - mu) * lax.rsqrt(var + EPS) * g + b
    return y * _sigmoid(y)


def _conv_prompt_kernel(cur_ref, halo_ref, w_ref, bdw_ref, g_ref, b_ref, o_ref,
                        ext_sc, sh_sc, wb_sc, c_sc, *, width, tc, halo, cblk):
    i = pl.program_id(0)
    ch = cur_ref.shape[1]
    rb = 16
    ext_sc[pl.ds(0, halo), :] = jnp.where(i > 0, halo_ref[...], 0.0)
    ext_sc[pl.ds(halo, tc), :] = cur_ref[...]
    ext_sc[pl.ds(halo + tc, 8), :] = jnp.zeros((8, ch), _f32)
    lead = halo - (width - 1)
    for cb in range(ch // cblk):
        cs = pl.ds(cb * cblk, cblk)
        for r in range(8):
            sh_sc[r] = ext_sc[pl.ds(r, halo + tc), cs]
        for j in range(width):
            wb_sc[j] = jnp.broadcast_to(w_ref[pl.ds(j, 1), cs], (8, cblk))
        bias = jnp.broadcast_to(bdw_ref[:, cs], (8, cblk))

        def rows(t, _):
            t0 = pl.multiple_of(t * rb, rb)
            accs = [bias] * (rb // 8)
            for j in range(width):
                off = lead + j
                wv = wb_sc[j]
                for a in range(rb // 8):
                    x = sh_sc[off % 8, pl.ds(t0 + a * 8 + (off // 8) * 8, 8), :]
                    accs[a] = accs[a] + x * wv
            for a in range(rb // 8):
                c_sc[pl.ds(t0 + a * 8, 8), cs] = accs[a]
            return 0

        lax.fori_loop(0, tc // rb, rows, 0)
    o_ref[...] = _ln_swish(c_sc[...], g_ref[...], b_ref[...]).astype(_bf16)


def _conv_prompt(glu, tp, w_dw, b_dw, ln_g, ln_b):
    ch = glu.shape[1]
    width = w_dw.shape[0]
    tc = Q_TILE
    halo = 32
    assert width - 1 <= halo and tp % tc == 0 and tc % halo == 0
    cblk = min(512, ch)
    vec = pl.BlockSpec((1, ch), lambda i: (0, 0))
    return pl.pallas_call(
        functools.partial(_conv_prompt_kernel, width=width, tc=tc, halo=halo, cblk=cblk),
        grid=(tp // tc,),
        in_specs=[pl.BlockSpec((tc, ch), lambda i: (i, 0)),
                  pl.BlockSpec((halo, ch), lambda i: (jnp.maximum(i * (tc // halo) - 1, 0), 0)),
                  pl.BlockSpec((width, ch), lambda i: (0, 0)),
                  vec, vec, vec],
        out_specs=pl.BlockSpec((tc, ch), lambda i: (i, 0)),
        out_shape=jax.ShapeDtypeStruct((tp, ch), _bf16),
        scratch_shapes=[pltpu.VMEM((halo + tc + 8, ch), _f32),
                        pltpu.VMEM((8, halo + tc, cblk), _f32),
                        pltpu.VMEM((width, 8, cblk), _f32),
                        pltpu.VMEM((tc, ch), _f32)],
        compiler_params=_cparams(("parallel",)),
        name="conv_prompt",
    )(glu, glu, w_dw, b_dw, ln_g, ln_b)


def _conv_sample_kernel(ext_ref, w_ref, bdw_ref, g_ref, b_ref, o_ref, *, width, ds):
    ch = ext_ref.shape[2]
    acc = jnp.broadcast_to(bdw_ref[...], (ds, ch))
    for j in range(width):
        acc = acc + ext_ref[0, pl.ds(j, ds), :] * w_ref[pl.ds(j, 1), :]
    o_ref[0] = _ln_swish(acc, g_ref[...], b_ref[...]).astype(_bf16)


def _conv_sample(ext, w_dw, b_dw, ln_g, ln_b, ds):
    db, rows, ch = ext.shape
    width = w_dw.shape[0]
    vec = pl.BlockSpec((1, ch), lambda b: (0, 0))
    return pl.pallas_call(
        functools.partial(_conv_sample_kernel, width=width, ds=ds),
        grid=(db,),
        in_specs=[pl.BlockSpec((1, rows, ch), lambda b: (b, 0, 0)),
                  pl.BlockSpec((width, ch), lambda b: (0, 0)),
                  vec, vec, vec],
        out_specs=pl.BlockSpec((1, ds, ch), lambda b: (b, 0, 0)),
        out_shape=jax.ShapeDtypeStruct((db, ds, ch), _bf16),
        compiler_params=_cparams(("parallel",)),
        name="conv_sample",
    )(ext, w_dw, b_dw, ln_g, ln_b)


def _merge_kernel(ca_ref, ao_ref, wco_ref, wo_ref, bco_ref, sgc_ref, sga_ref, m_ref):
    yc = _dot(ca_ref[...], wco_ref[...]) + bco_ref[...]
    ya = _dot(ao_ref[...], wo_ref[...])
    m_ref[...] = (sgc_ref[...].astype(_f32) * yc + sga_ref[...].astype(_f32) * ya).astype(_bf16)


def _merge(ca, ao, wco, wo, bco, sg):
    mp, ch = ca.shape
    aw = ao.shape[1]
    d = wco.shape[1]
    tm = _row_tile(mp, PROJ_ROW_CAP)
    tn = min(512, d)
    nj = d // tn
    return pl.pallas_call(
        _merge_kernel,
        grid=(mp // tm, nj),
        in_specs=[pl.BlockSpec((tm, ch), lambda i, j: (i, 0)),
                  pl.BlockSpec((tm, aw), lambda i, j: (i, 0)),
                  pl.BlockSpec((ch, tn), lambda i, j: (0, j)),
                  pl.BlockSpec((aw, tn), lambda i, j: (0, j)),
                  pl.BlockSpec((1, tn), lambda i, j: (0, j)),
                  pl.BlockSpec((tm, tn), lambda i, j: (i, j)),
                  pl.BlockSpec((tm, tn), lambda i, j: (i, j + nj))],
        out_specs=pl.BlockSpec((tm, tn), lambda i, j: (i, j)),
        out_shape=jax.ShapeDtypeStruct((mp, d), _bf16),
        compiler_params=_cparams(("parallel", "parallel")),
        name="merge",
    )(ca, ao, wco, wo, bco, sg, sg)


def _resid_kernel(m_ref, w_ref, h_ref, o_ref):
    o_ref[...] = h_ref[...] + _dot(m_ref[...], w_ref[...])


def _resid_proj(m, w, h):
    mp, d = m.shape
    n = w.shape[1]
    tm, tn = _row_tile(mp, PROJ_ROW_CAP), _col_tile(n)
    return pl.pallas_call(
        _resid_kernel,
        grid=(mp // tm, n // tn),
        in_specs=[pl.BlockSpec((tm, d), lambda i, j: (i, 0)),
                  pl.BlockSpec((d, tn), lambda i, j: (0, j)),
                  pl.BlockSpec((tm, tn), lambda i, j: (i, j))],
        out_specs=pl.BlockSpec((tm, tn), lambda i, j: (i, j)),
        out_shape=jax.ShapeDtypeStruct((mp, n), _f32),
        compiler_params=_cparams(("parallel", "parallel")),
        name="resid_proj",
    )(m, w, h)


def _order_key(s):
    b = lax.bitcast_convert_type(s, jnp.int32)
    return b ^ (lax.shift_right_arithmetic(b, 31) & jnp.int32(0x7FFFFFFF))


def _bit_search(count_ge, rows, n_bits, k, lowest):
    lowest = jnp.int32(lowest)

    def body(b, t_u):
        bit = lax.shift_left(jnp.int32(1), jnp.int32(n_bits - 1) - jnp.asarray(b, jnp.int32))
        cand = t_u | bit
        cnt = count_ge(cand + lowest)
        return jnp.where(cnt >= k, cand, t_u)

    t_u = lax.fori_loop(0, n_bits, body, jnp.zeros((rows, 1), jnp.int32))
    return t_u + lowest


def _kth_key_packed(load, pk_ref, scan, rows, width, k):
    halves = width // LANES
    i16 = jnp.int16
    half_min, half_max = -(2 ** 15), 2 ** 15 - 1

    def count16(th):
        th16 = th.astype(i16)

        def one(c, acc):
            w = jnp.where(pk_ref[c] >= th16, i16(1), i16(0))
            for j in range(halves):
                acc = acc + w[:, j * LANES:(j + 1) * LANES]
            return acc
        acc = scan(one, jnp.zeros((rows, LANES), i16))
        return jnp.sum(acc.astype(_f32), axis=1, keepdims=True)

    def fill_top(c, carry):
        pk_ref[c] = lax.shift_right_arithmetic(load(c), 16).astype(i16)
        return carry
    scan(fill_top, 0)
    top = _bit_search(count16, rows, 16, k, half_min)
    above = jnp.where(top >= half_max, 0.0, count16(jnp.minimum(top + 1, half_max)))

    def fill_low(c, carry):
        x = load(c)
        low = (x & jnp.int32(0xFFFF)) + half_min
        pk_ref[c] = jnp.where(lax.shift_right_arithmetic(x, 16) == top, low, half_min).astype(i16)
        return carry
    scan(fill_low, 0)
    low = _bit_search(count16, rows, 16, k - above, half_min)
    return lax.shift_left(top, 16) | (low - half_min)


def _select_threshold(load, store, n_iters, unroll, rows, width, k, idx_bits, pk_ref=None):
    halves = width // LANES

    def scan(fn, init):
        def body(it, carry):
            for u in range(unroll):
                carry = fn(it * unroll + u, carry)
            return carry
        return lax.fori_loop(0, n_iters, body, init)

    def count(pred):
        def one(c, acc):
            m = pred(load(c), c)
            for j in range(halves):
                acc = acc + jnp.where(m[:, j * LANES:(j + 1) * LANES], 1.0, 0.0)
            return acc
        return jnp.sum(scan(one, jnp.zeros((rows, LANES), _f32)), axis=1, keepdims=True)

    kf = jnp.float32(k)
    if pk_ref is None:
        t = _bit_search(lambda th: count(lambda x, c: x >= th), rows, 32, kf, INT_MIN)
    else:
        t = _kth_key_packed(load, pk_ref, scan, rows, width, kf)
    t = jnp.maximum(t, jnp.int32(INT_MIN + 1))
    n_ge = count(lambda x, c: x >= t)

    @pl.when(jnp.max(n_ge) > kf)
    def _():
        n_gt = count(lambda x, c: x > t)
        need = kf - n_gt
        lane = lax.broadcasted_iota(jnp.int32, (rows, width), 1)

        def ties_below(j):
            return count(lambda x, c: (x == t) & ((lane + c * width) < j))

        j_cut = _bit_search(lambda j: need - ties_below(j) + (kf - 1.0), rows, idx_bits, kf, 0)
        surplus = n_ge > kf

        def rewrite(c, carry):
            x = load(c)
            drop = (x == t) & ((lane + c * width) > j_cut) & surplus
            store(c, jnp.where(drop, jnp.int32(INT_MIN), x))
            return carry
        scan(rewrite, 0)

    return t


def _prompt_attn_kernel(iq_ref, iw_ref, q_ref, ikt_ref, kt_ref, v_ref, o_ref,
                        key_sc, pk_sc, thr_sc, wb_sc, m_sc, acc_sc,
                        *, nih, n_heads, n_kv, topk, idx_bits, sub):
    i = pl.program_id(0)
    tq = q_ref.shape[0]
    group = n_heads // n_kv
    unroll = 2

    for h in range(nih):
        wb_sc[h] = jnp.broadcast_to(iw_ref[:, h:h + 1], (tq, LANES))

    def scores(c):
        rhs = ikt_ref[c]
        halves = [jnp.zeros((tq, LANES), _f32) for _ in range(tq // LANES)]
        for h in range(nih):
            d = _dot(iq_ref[:, h * LANES:(h + 1) * LANES], rhs)
            w = wb_sc[h]
            for j in range(len(halves)):
                halves[j] = halves[j] + jnp.maximum(d[:, j * LANES:(j + 1) * LANES], 0.0) * w
        return _order_key(jnp.concatenate(halves, axis=1))

    def score_body(c, _):
        key_sc[c] = scores(c)
        return 0

    lax.fori_loop(0, i, score_body, 0)
    row = lax.broadcasted_iota(jnp.int32, (tq, tq), 0)
    col = lax.broadcasted_iota(jnp.int32, (tq, tq), 1)
    key_sc[i] = jnp.where(col <= row, scores(i), jnp.int32(INT_MIN))
    key_sc[i + 1] = jnp.full((tq, tq), INT_MIN, jnp.int32)
    n_iters = lax.shift_right_logical(i + unroll, 1)

    for rb in range(tq // sub):
        rs = pl.ds(rb * sub, sub)
        t = _select_threshold(lambda c: key_sc[c, rs, :],
                              lambda c, x: key_sc.__setitem__((c, rs, slice(None)), x),
                              n_iters, unroll, sub, tq, topk, idx_bits, pk_ref=pk_sc)
        thr_sc[rs, :] = jnp.broadcast_to(t, (sub, LANES))

    thr = jnp.concatenate([thr_sc[...]] * (tq // LANES), axis=1)
    m_sc[...] = jnp.full(m_sc.shape, NEG, _f32)
    acc_sc[...] = jnp.zeros_like(acc_sc)
    ones = jnp.ones((tq, LANES), _bf16)

    def attend(c, _):
        bias = jnp.where(key_sc[c] >= thr, 0.0, NEG)
        vc = v_ref[c]

        for n in range(n_kv):
            kt = kt_ref[c, n]
            vn1 = jnp.concatenate([vc[:, n * LANES:(n + 1) * LANES], ones], axis=1)
            for g in range(group):
                h = n * group + g
                lg = _dot(q_ref[:, h * LANES:(h + 1) * LANES], kt) + bias
                m_prev = m_sc[h]
                m_new = jnp.maximum(m_prev, jnp.max(lg, axis=1, keepdims=True))
                alpha = jnp.exp2(m_prev - m_new)
                p = jnp.exp2(lg - jnp.concatenate([m_new] * (tq // LANES), axis=1))
                acc_sc[h] = jnp.concatenate([alpha, alpha], axis=1) * acc_sc[h] + _dot(p.astype(_bf16), vn1)
                m_sc[h] = m_new
        return 0

    lax.fori_loop(0, i + 1, attend, 0)
    for h in range(n_heads):
        a = acc_sc[h]
        o_ref[:, h * LANES:(h + 1) * LANES] = (a[:, :LANES] / a[:, LANES:]).astype(_bf16)


def _prompt_attention(iq, iw, q, ikt, kt, v, *, tp, nih, n_heads, n_kv, topk):
    tq = Q_TILE
    nc = tp // tq
    resident = lambda shape: pl.BlockSpec(shape, lambda i: (0,) * len(shape),
                                          pipeline_mode=pl.Buffered(1))
    idx_bits = max(1, int(np.ceil(np.log2(tp + 1))))
    sub = min(SEARCH_ROWS, tq)
    return pl.pallas_call(
        functools.partial(_prompt_attn_kernel, nih=nih, n_heads=n_heads, n_kv=n_kv, topk=topk,
                          idx_bits=idx_bits, sub=sub),
        grid=(nc,),
        in_specs=[pl.BlockSpec((tq, nih * LANES), lambda i: (i, 0)),
                  pl.BlockSpec((tq, nih), lambda i: (i, 0)),
                  pl.BlockSpec((tq, n_heads * LANES), lambda i: (i, 0)),
                  resident((nc, LANES, tq)),
                  resident((nc, n_kv, LANES, tq)),
                  resident((nc, tq, n_kv * LANES))],
        out_specs=pl.BlockSpec((tq, n_heads * LANES), lambda i: (i, 0)),
        out_shape=jax.ShapeDtypeStruct((tp, n_heads * LANES), _bf16),
        scratch_shapes=[pltpu.VMEM((nc + 1, tq, tq), jnp.int32),
                        pltpu.VMEM((nc + 1, sub, tq), jnp.int16),
                        pltpu.VMEM((tq, LANES), jnp.int32),
                        pltpu.VMEM((nih, tq, LANES), _f32),
                        pltpu.VMEM((n_heads, tq, LANES), _f32),
                        pltpu.VMEM((n_heads, tq, 2 * LANES), _f32)],
        compiler_params=_cparams(("arbitrary",)),
        name="prompt_attention",
    )(iq, iw, q, ikt, kt, v)


def _ring_slot(g):
    return g % RING_SLOTS if isinstance(g, int) else lax.rem(g, RING_SLOTS)


def _ring_copies(pt_ref, streams, sem_ref, pps, g):
    slot = _ring_slot(g)
    return [pltpu.make_async_copy(cache.at[pt_ref[g * pps + j]], buf.at[slot, j], sem_ref.at[slot])
            for j in range(pps) for cache, buf in streams]


def _ring_step(pt_ref, streams, sem_ref, pps, g, n_total):
    ahead = RING_SLOTS - 1

    def request(step):
        for n, cp in enumerate(_ring_copies(pt_ref, streams, sem_ref, pps, step)):
            cp.start(priority=n % 2)

    @pl.when(g == 0)
    def _():
        for step in range(min(ahead, n_total)):
            request(step)

    @pl.when(g + ahead < n_total)
    def _():
        request(g + ahead)

    for cp in _ring_copies(pt_ref, streams, sem_ref, pps, g):
        cp.wait()


def _sample_index_kernel(pt_ref, iq_ref, iw_ref, ikn_ref, cache_ref, key_ref, thr_ref, buf, sem,
                         *, pps, n_steps, n_total, n_pages, n_chunks, ds, nih, topk, idx_bits):
    s = pl.program_id(1)
    g = pl.program_id(0) * n_steps + s
    _ring_step(pt_ref, [(cache_ref, buf)], sem, pps, g, n_total)
    slot = _ring_slot(g)

    def index_keys(ik_rows):
        n_keys = ik_rows.shape[0]
        w = jnp.concatenate([iw_ref[0]] * (n_keys // LANES), axis=1)
        x = jnp.maximum(_dot_nt(iq_ref[0], ik_rows), 0.0) * w
        return _order_key(jnp.sum(x.reshape(nih, ds, n_keys), axis=0))

    pair = 2 if pps % 2 == 0 else 1
    for j in range(0, pps, pair):
        pages = jnp.concatenate([buf[slot, j + a].astype(_bf16) for a in range(pair)], axis=0)
        keys = index_keys(pages)
        for a in range(pair):
            key_ref[0, s * pps + j + a] = keys[:, a * LANES:(a + 1) * LANES]

    @pl.when(s == n_steps - 1)
    def _():
        qi = lax.broadcasted_iota(jnp.int32, (ds, LANES), 0)
        ki = lax.broadcasted_iota(jnp.int32, (ds, LANES), 1)
        key_ref[0, n_pages] = jnp.where(ki <= qi, index_keys(ikn_ref[0]), jnp.int32(INT_MIN))
        for c in range(n_pages + 1, n_chunks):
            key_ref[0, c] = jnp.full((ds, LANES), INT_MIN, jnp.int32)
        t = _select_threshold(lambda c: key_ref[0, c],
                              lambda c, x: key_ref.__setitem__((0, c), x),
                              n_chunks // SEARCH_UNROLL_SAMPLE, SEARCH_UNROLL_SAMPLE,
                              ds, LANES, topk, idx_bits)
        thr_ref[0] = jnp.broadcast_to(t, (ds, LANES))


def _sample_index(page_table, iq_r, iw_r, ik_new, cache_ik, *, ds, nih, topk):
    db, n_pages = page_table.shape
    pps = min(PAGES_PER_STEP_INDEX, n_pages)
    assert n_pages % pps == 0
    n_steps = n_pages // pps
    page = cache_ik.shape[1]
    n_chunks = _round_up(n_pages + 1, SEARCH_UNROLL_SAMPLE)
    rows_i = nih * ds
    idx_bits = max(1, int(np.ceil(np.log2(n_chunks * LANES + 1))))
    per_b = lambda shape: pl.BlockSpec((1,) + shape, lambda b, s, pt: (b,) + (0,) * len(shape))

    grid_spec = pltpu.PrefetchScalarGridSpec(
        num_scalar_prefetch=1,
        grid=(db, n_steps),
        in_specs=[per_b((rows_i, LANES)), per_b((rows_i, LANES)), per_b((LANES, LANES)),
                  pl.BlockSpec(memory_space=pl.ANY)],
        out_specs=(per_b((n_chunks, ds, LANES)), per_b((ds, LANES))),
        scratch_shapes=[pltpu.VMEM((RING_SLOTS, pps, page, LANES), _f32),
                        pltpu.SemaphoreType.DMA((RING_SLOTS,))],
    )
    return pl.pallas_call(
        functools.partial(_sample_index_kernel, pps=pps, n_steps=n_steps, n_total=db * n_steps, n_pages=n_pages,
                          n_chunks=n_chunks, ds=ds, nih=nih, topk=topk, idx_bits=idx_bits),
        grid_spec=grid_spec,
        out_shape=(jax.ShapeDtypeStruct((db, n_chunks, ds, LANES), jnp.int32),
                   jax.ShapeDtypeStruct((db, ds, LANES), jnp.int32)),
        compiler_params=_cparams(("arbitrary", "arbitrary")),
        name="sample_index",
    )(page_table.reshape(-1), iq_r, iw_r, ik_new, cache_ik)


def _sample_attend_kernel(pt_ref, q_ref, key_ref, thr_ref, kn_ref, vn_ref, ck_ref, cv_ref, o_ref,
                          kbuf, vbuf, sem, m_sc, l_sc, acc_sc,
                          *, pps, n_steps, n_total, n_pages, ds, n_heads, n_kv):
    s = pl.program_id(1)
    g = pl.program_id(0) * n_steps + s
    _ring_step(pt_ref, [(ck_ref, kbuf), (cv_ref, vbuf)], sem, pps, g, n_total)
    slot = _ring_slot(g)
    group = n_heads // n_kv
    rows_g = group * ds
    page = kbuf.shape[2] // n_kv

    @pl.when(s == 0)
    def _():
        m_sc[...] = jnp.full(m_sc.shape, NEG, _f32)
        l_sc[...] = jnp.zeros_like(l_sc)
        acc_sc[...] = jnp.zeros_like(acc_sc)

    def attend(chunk_ids, k_of, v_of):
        thr = thr_ref[0]
        bias = jnp.concatenate([jnp.where(key_ref[0, c] >= thr, 0.0, NEG) for c in chunk_ids], axis=1)
        bias = jnp.concatenate([bias] * group, axis=0)
        m_prev, l_prev, acc_prev = m_sc[...], l_sc[...], acc_sc[...]
        heads = [slice(n * rows_g, (n + 1) * rows_g) for n in range(n_kv)]
        lgs = [_dot_nt(q_ref[0, rs, :],
                       jnp.concatenate([k_of(j, n) for j in range(len(chunk_ids))], axis=0)) + bias
               for n, rs in enumerate(heads)]
        m_out = [jnp.maximum(m_prev[rs], jnp.max(lg, axis=1, keepdims=True)) for lg, rs in zip(lgs, heads)]
        ps = [jnp.exp2(lg - m_new[:, :1]) for lg, m_new in zip(lgs, m_out)]
        pvs = [_dot(p.astype(_bf16), jnp.concatenate([v_of(j, n) for j in range(len(chunk_ids))], axis=0))
               for n, p in enumerate(ps)]
        l_out, acc_out = [], []
        for rs, m_new, p, pv in zip(heads, m_out, ps, pvs):
            alpha = jnp.exp2(m_prev[rs] - m_new)
            l_out.append(alpha * l_prev[rs] + jnp.sum(p, axis=1, keepdims=True))
            acc_out.append(alpha * acc_prev[rs] + pv)
        m_sc[...] = jnp.concatenate(m_out, axis=0)
        l_sc[...] = jnp.concatenate(l_out, axis=0)
        acc_sc[...] = jnp.concatenate(acc_out, axis=0)

    def head_rows(buf, j, n):
        return buf[slot, j, pl.ds(n, page, stride=n_kv), :].astype(_bf16)

    attend([s * pps + j for j in range(pps)],
           lambda j, n: head_rows(kbuf, j, n), lambda j, n: head_rows(vbuf, j, n))

    @pl.when(s == n_steps - 1)
    def _():
        attend([n_pages],
               lambda j, n: kn_ref[0, :, n * LANES:(n + 1) * LANES],
               lambda j, n: vn_ref[0, :, n * LANES:(n + 1) * LANES])
        o_ref[0] = (acc_sc[...] / l_sc[...]).astype(_bf16)


def _sample_attend(page_table, q_r, keys, thr, k_new, v_new, cache_k, cache_v, *, ds, n_heads, n_kv):
    db, n_pages = page_table.shape
    pps = min(PAGES_PER_STEP_ATTEND, n_pages)
    assert n_pages % pps == 0
    n_steps = n_pages // pps
    n_chunks = keys.shape[1]
    rows_q = n_heads * ds
    kvw = n_kv * LANES
    page_rows = cache_k.shape[1]
    per_b = lambda shape: pl.BlockSpec((1,) + shape, lambda b, s, pt: (b,) + (0,) * len(shape))
    hbm = pl.BlockSpec(memory_space=pl.ANY)

    grid_spec = pltpu.PrefetchScalarGridSpec(
        num_scalar_prefetch=1,
        grid=(db, n_steps),
        in_specs=[per_b((rows_q, LANES)), per_b((n_chunks, ds, LANES)), per_b((ds, LANES)),
                  per_b((LANES, kvw)), per_b((LANES, kvw)), hbm, hbm],
        out_specs=per_b((rows_q, LANES)),
        scratch_shapes=[pltpu.VMEM((RING_SLOTS, pps, page_rows, LANES), _f32),
                        pltpu.VMEM((RING_SLOTS, pps, page_rows, LANES), _f32),
                        pltpu.SemaphoreType.DMA((RING_SLOTS,)),
                        pltpu.VMEM((rows_q, LANES), _f32),
                        pltpu.VMEM((rows_q, LANES), _f32),
                        pltpu.VMEM((rows_q, LANES), _f32)],
    )
    return pl.pallas_call(
        functools.partial(_sample_attend_kernel, pps=pps, n_steps=n_steps, n_total=db * n_steps,
                          n_pages=n_pages, ds=ds,
                          n_heads=n_heads, n_kv=n_kv),
        grid_spec=grid_spec,
        out_shape=jax.ShapeDtypeStruct((db, rows_q, LANES), _bf16),
        compiler_params=_cparams(("arbitrary", "arbitrary")),
        name="sample_attend",
    )(page_table.reshape(-1), q_r, keys, thr, k_new, v_new, cache_k, cache_v)


def _rope_tables(pos, dim):
    half = dim // 2
    inv = ROPE_THETA ** (-jnp.arange(half, dtype=_f32) / half)
    ang = pos.astype(_f32)[:, None] * inv[None, :]
    cos, sin = jnp.cos(ang), jnp.sin(ang)
    return jnp.concatenate([cos, cos], axis=1), jnp.concatenate([-sin, sin], axis=1)


def kernel(x_prompt, x_sample, cache_k, cache_v, cache_ik, state_conv, page_table, meta,
           norm1_g, ffn1_w1, ffn1_w3, ffn1_w2, norm_mix_g, w_in, w_dw, b_dw, conv_ln_g, conv_ln_b,
           w_conv_out, b_conv_out, w_o, w_out, norm2_g, ffn2_w1, ffn2_w3, ffn2_w2, final_g):
    bsz, seq, d = x_prompt.shape
    db, ds, _ = x_sample.shape
    depth, n_pool, page, n_kv, hd = cache_k.shape
    idx_dim = cache_ik.shape[-1]
    n_meta = meta.shape[0]
    width = w_dw.shape[1]
    ch = w_dw.shape[2]
    attn_w = w_o.shape[1]
    n_heads = attn_w // hd
    kv_w = n_kv * hd
    in_w = w_in.shape[2]
    nih = (in_w - 2 * ch - attn_w - 2 * kv_w - idx_dim - 2 * d) // (idx_dim + 1)
    n_pages = page_table.shape[1]
    past = n_pages * page
    assert bsz == 1 and depth == 1 and hd == LANES and idx_dim == LANES and page == LANES
    assert 2 * ch + attn_w + 2 * kv_w + nih * idx_dim + idx_dim + nih + 2 * d == in_w
    assert (n_heads * ds) % 8 == 0 and ds % 8 == 0 and ds <= LANES

    t = n_meta + seq
    tp = _round_up(t, Q_TILE)
    ms = db * ds
    mp = _round_up(tp + ms, FFN_ROW_TILE)
    topk_p = min(MAX_TOPK, seq // 4)
    topk_s = min(MAX_TOPK, (past + ds) // 4)
    assert topk_p <= Q_TILE and topk_s <= LANES * (n_pages + 1)

    h0 = jnp.concatenate([meta, x_prompt[0], jnp.zeros((tp - t, d), _f32),
                          x_sample.reshape(ms, d), jnp.zeros((mp - tp - ms, d), _f32)], axis=0)
    pos = jnp.concatenate([jnp.arange(t, dtype=jnp.int32), jnp.zeros((tp - t,), jnp.int32),
                           jnp.tile(past + jnp.arange(ds, dtype=jnp.int32), db),
                           jnp.zeros((mp - tp - ms,), jnp.int32)])
    cos, sin = _rope_tables(pos, hd)

    vec = lambda a: a.reshape(1, -1).astype(_f32)
    cast = lambda a: a.astype(_bf16)
    offs = np.cumsum([0, ch, ch, attn_w, kv_w, kv_w, nih * idx_dim, idx_dim, nih, d, d])
    wi = w_in[0]
    cols = lambda a, b: cast(wi[:, offs[a]:offs[b]])

    h1, u = _ffn(h0, vec(norm1_g[0]), cast(ffn1_w1[0]), cast(ffn1_w3[0]), cast(ffn1_w2[0]),
                 vec(norm_mix_g[0]), emit_h=True)

    glu = _proj_glu(u, cols(0, 1), cols(1, 2))
    (q,) = _proj(u, cols(2, 3), mode="rope", cos=cos, sin=sin, scale=hd ** -0.5 * LOG2E,
                 out_dtypes=(_bf16,), name="proj_q")
    k32, k16 = _proj(u, cols(3, 4), mode="rope", cos=cos, sin=sin, out_dtypes=(_f32, _bf16), name="proj_k")
    v32, v16 = _proj(u, cols(4, 5), out_dtypes=(_f32, _bf16), name="proj_v")
    (iq,) = _proj(u, cols(5, 6), mode="rope", cos=cos, sin=sin, out_dtypes=(_bf16,), name="proj_iq")
    ik32, ik16 = _proj(u, cols(6, 7), mode="rope", cos=cos, sin=sin, out_dtypes=(_f32, _bf16), name="proj_ik")
    (iw,) = _proj(u, cols(7, 8), scale=(idx_dim ** -0.5) * (nih ** -0.5), name="proj_iw")
    (sg,) = _proj(u, cols(8, 10), mode="sigmoid", out_dtypes=(_bf16,), name="proj_gates")

    ca_p = _conv_prompt(glu, tp, w_dw[0], vec(b_dw[0]), vec(conv_ln_g[0]), vec(conv_ln_b[0]))
    glu_s = glu[tp:tp + ms].reshape(db, ds, ch)
    ext_rows = _round_up(width - 1 + ds, 8)
    ext_s = jnp.concatenate([state_conv[0], glu_s,
                             jnp.zeros((db, ext_rows - (width - 1 + ds), ch), _f32)], axis=1)
    ca_s = _conv_sample(ext_s, w_dw[0], vec(b_dw[0]), vec(conv_ln_g[0]), vec(conv_ln_b[0]), ds)
    ca = jnp.concatenate([ca_p, ca_s.reshape(ms, ch), jnp.zeros((mp - tp - ms, ch), _bf16)], axis=0)

    nc = tp // Q_TILE
    ikt = ik16[:tp].reshape(nc, Q_TILE, idx_dim).transpose(0, 2, 1)
    kt = k16[:tp].reshape(nc, Q_TILE, n_kv, hd).transpose(0, 2, 3, 1)
    vch = v16[:tp].reshape(nc, Q_TILE, kv_w)
    ao_p = _prompt_attention(iq, iw, q, ikt, kt, vch, tp=tp, nih=nih, n_heads=n_heads, n_kv=n_kv,
                             topk=topk_p)

    sl = slice(tp, tp + ms)
    iq_r = iq[sl].reshape(db, ds, nih, idx_dim).transpose(0, 2, 1, 3).reshape(db, nih * ds, idx_dim)
    iw_r = jnp.broadcast_to(iw[sl].reshape(db, ds, nih).transpose(0, 2, 1).reshape(db, nih * ds, 1),
                            (db, nih * ds, LANES))
    q_r = q[sl].reshape(db, ds, n_heads, hd).transpose(0, 2, 1, 3).reshape(db, n_heads * ds, hd)
    pad_keys = lambda a: jnp.pad(a.reshape(db, ds, -1), ((0, 0), (0, LANES - ds), (0, 0)))
    keys_s, thr_s = _sample_index(page_table, iq_r, iw_r, pad_keys(ik16[sl]), cache_ik[0],
                                  ds=ds, nih=nih, topk=topk_s)
    ao_s = _sample_attend(page_table, q_r, keys_s, thr_s, pad_keys(k16[sl]), pad_keys(v16[sl]),
                          cache_k[0].reshape(n_pool, page * n_kv, hd),
                          cache_v[0].reshape(n_pool, page * n_kv, hd),
                          ds=ds, n_heads=n_heads, n_kv=n_kv)
    ao_s = ao_s.reshape(db, n_heads, ds, hd).transpose(0, 2, 1, 3).reshape(ms, attn_w)
    ao = jnp.concatenate([ao_p, ao_s, jnp.zeros((mp - tp - ms, attn_w), _bf16)], axis=0)

    m = _merge(ca, ao, cast(w_conv_out[0]), cast(w_o[0]), vec(b_conv_out[0]), sg)
    h2 = _resid_proj(m, cast(w_out[0]), h1)
    y = _ffn(h2, vec(norm2_g[0]), cast(ffn2_w1[0]), cast(ffn2_w3[0]), cast(ffn2_w2[0]),
             vec(final_g), emit_h=False)

    y_prompt = y[n_meta:t][None]
    y_sample = y[sl].reshape(db, ds, d)
    hist = width - 1
    new_conv_p = glu[t - hist:t][None, None]
    new_conv_s = jnp.concatenate([state_conv[0], glu_s], axis=1)[:, -hist:][None]
    return (y_prompt, y_sample,
            k32[:t].reshape(1, 1, t, n_kv, hd), v32[:t].reshape(1, 1, t, n_kv, hd),
            ik32[:t].reshape(1, 1, t, idx_dim), new_conv_p,
            k32[sl].reshape(1, db, ds, n_kv, hd), v32[sl].reshape(1, db, ds, n_kv, hd),
            ik32[sl].reshape(1, db, ds, idx_dim), new_conv_s)
```

```python
import functools

import numpy as np
import jax
import jax.numpy as jnp
from jax import lax
from jax.experimental import pallas as pl
from jax.experimental.pallas import tpu as pltpu

MAX_TOPK = 256
ROPE_THETA = 10000.0
EPS = 1e-6

LANES = 128
FFN_ROW_TILE = 512
FF_TILE = 512
PROJ_ROW_CAP = 1088
PROJ_COL_TILE = 1024
Q_TILE = 256
PAGES_PER_STEP_INDEX = 64
PAGES_PER_STEP_ATTEND = 16
RING_SLOTS = 4
SEARCH_UNROLL_SAMPLE = 16
SEARCH_ROWS = 128
VMEM_LIMIT = 56 * 1024 * 1024
INT_MIN = -(2 ** 31)
NEG = -1e30
LOG2E = 1.4426950408889634

_bf16 = jnp.bfloat16
_f32 = jnp.float32


def _cparams(sem):
    return pltpu.CompilerParams(dimension_semantics=sem, vmem_limit_bytes=VMEM_LIMIT)


def _dot(a, b):
    return jnp.dot(a, b, preferred_element_type=_f32)


def _dot_nt(a, b):
    return lax.dot_general(a, b, (((1,), (1,)), ((), ())), preferred_element_type=_f32)


def _rms(x, g):
    return x * lax.rsqrt(jnp.mean(x * x, axis=-1, keepdims=True) + EPS) * g


def _sigmoid(x):
    return 1.0 / (1.0 + jnp.exp(-x))


def _round_up(x, m):
    return (x + m - 1) // m * m


def _row_tile(rows, cap):
    return max(t for t in range(16, cap + 1, 16) if rows % t == 0)


def _col_tile(n):
    return PROJ_COL_TILE if n % PROJ_COL_TILE == 0 else (512 if n % 512 == 0 else n)


def _ffn_kernel(h_ref, g_ref, w1_ref, w3_ref, w2_ref, g2_ref, *refs, n_f, emit_h):
    if emit_h:
        h_out_ref, u_out_ref, u_sc, acc_sc = refs
    else:
        y_out_ref, u_sc, acc_sc = refs
    f = pl.program_id(1)

    @pl.when(f == 0)
    def _():
        u_sc[...] = _rms(h_ref[...], g_ref[...]).astype(_bf16)
        acc_sc[...] = jnp.zeros_like(acc_sc)

    u = u_sc[...]
    a = _dot(u, w1_ref[...])
    b = _dot(u, w3_ref[...])
    gate = (a * _sigmoid(a)) * b
    acc_sc[...] += _dot(gate.astype(_bf16), w2_ref[...])

    @pl.when(f == n_f - 1)
    def _():
        hn = h_ref[...] + 0.5 * acc_sc[...]
        un = _rms(hn, g2_ref[...])
        if emit_h:
            h_out_ref[...] = hn
            u_out_ref[...] = un.astype(_bf16)
        else:
            y_out_ref[...] = un


def _ffn(h, g, w1, w3, w2, g2, *, emit_h):
    mp, d = h.shape
    dff = w1.shape[1]
    tm = FFN_ROW_TILE
    tf = min(FF_TILE, dff)
    assert mp % tm == 0 and dff % tf == 0
    n_f = dff // tf
    row = pl.BlockSpec((tm, d), lambda i, f: (i, 0))
    vec = pl.BlockSpec((1, d), lambda i, f: (0, 0))
    if emit_h:
        out_shape = (jax.ShapeDtypeStruct((mp, d), _f32), jax.ShapeDtypeStruct((mp, d), _bf16))
        out_specs = (row, row)
    else:
        out_shape = jax.ShapeDtypeStruct((mp, d), _f32)
        out_specs = row
    return pl.pallas_call(
        functools.partial(_ffn_kernel, n_f=n_f, emit_h=emit_h),
        grid=(mp // tm, n_f),
        in_specs=[row, vec,
                  pl.BlockSpec((d, tf), lambda i, f: (0, f)),
                  pl.BlockSpec((d, tf), lambda i, f: (0, f)),
                  pl.BlockSpec((tf, d), lambda i, f: (f, 0)),
                  vec],
        out_specs=out_specs,
        out_shape=out_shape,
        scratch_shapes=[pltpu.VMEM((tm, d), _bf16), pltpu.VMEM((tm, d), _f32)],
        compiler_params=_cparams(("parallel", "arbitrary")),
        name="ffn_emit_h" if emit_h else "ffn_final",
    )(h, g, w1, w3, w2, g2)


def _glu_kernel(u_ref, wa_ref, wb_ref, o_ref):
    u = u_ref[...]
    o_ref[...] = _dot(u, wa_ref[...]) * _sigmoid(_dot(u, wb_ref[...]))


def _proj_glu(u, wa, wb):
    mp, d = u.shape
    n = wa.shape[1]
    tm, tn = _row_tile(mp, PROJ_ROW_CAP), _col_tile(n)
    return pl.pallas_call(
        _glu_kernel,
        grid=(mp // tm, n // tn),
        in_specs=[pl.BlockSpec((tm, d), lambda i, j: (i, 0)),
                  pl.BlockSpec((d, tn), lambda i, j: (0, j)),
                  pl.BlockSpec((d, tn), lambda i, j: (0, j))],
        out_specs=pl.BlockSpec((tm, tn), lambda i, j: (i, j)),
        out_shape=jax.ShapeDtypeStruct((mp, n), _f32),
        compiler_params=_cparams(("parallel", "parallel")),
        name="proj_glu",
    )(u, wa, wb)


def _proj_kernel(u_ref, w_ref, *refs, mode, scale, out_dtypes):
    if mode == "rope":
        cos_ref, sin_ref = refs[:2]
        outs = refs[2:]
    else:
        outs = refs
    z = _dot(u_ref[...], w_ref[...])
    if mode == "rope":
        cos = cos_ref[...]
        sin = sin_ref[...]
        for j in range(z.shape[1] // LANES):
            cs = slice(j * LANES, (j + 1) * LANES)
            x = z[:, cs]
            r = x * cos + pltpu.roll(x, LANES // 2, 1) * sin
            if scale != 1.0:
                r = r * scale
            for o_ref, dt in zip(outs, out_dtypes):
                o_ref[:, cs] = r.astype(dt)
        return
    if mode == "sigmoid":
        z = _sigmoid(z)
    if scale != 1.0:
        z = z * scale
    for o_ref, dt in zip(outs, out_dtypes):
        o_ref[...] = z.astype(dt)


def _proj(u, w, *, mode="plain", cos=None, sin=None, scale=1.0, out_dtypes=(_f32,), name="proj"):
    mp, d = u.shape
    n = w.shape[1]
    tm, tn = _row_tile(mp, PROJ_ROW_CAP), _col_tile(n)
    in_specs = [pl.BlockSpec((tm, d), lambda i, j: (i, 0)),
                pl.BlockSpec((d, tn), lambda i, j: (0, j))]
    args = [u, w]
    if mode == "rope":
        in_specs += [pl.BlockSpec((tm, LANES), lambda i, j: (i, 0))] * 2
        args += [cos, sin]
    ospec = pl.BlockSpec((tm, tn), lambda i, j: (i, j))
    outs = pl.pallas_call(
        functools.partial(_proj_kernel, mode=mode, scale=scale, out_dtypes=out_dtypes),
        grid=(mp // tm, n // tn),
        in_specs=in_specs,
        out_specs=tuple(ospec for _ in out_dtypes),
        out_shape=tuple(jax.ShapeDtypeStruct((mp, n), dt) for dt in out_dtypes),
        compiler_params=_cparams(("parallel", "parallel")),
        name=name,
    )(*args)
    return outs


def _ln_swish(c, g, b):
    mu = jnp.mean(c, axis=-1, keepdims=True)
    var = jnp.mean(jnp.square(c - mu), axis=-1, keepdims=True)
    y = (c - mu) * lax.rsqrt(var + EPS) * g + b
    return y * _sigmoid(y)


def _conv_prompt_kernel(cur_ref, halo_ref, w_ref, bdw_ref, g_ref, b_ref, o_ref,
                        ext_sc, sh_sc, wb_sc, c_sc, *, width, tc, halo, cblk):
    i = pl.program_id(0)
    ch = cur_ref.shape[1]
    rb = 16
    ext_sc[pl.ds(0, halo), :] = jnp.where(i > 0, halo_ref[...], 0.0)
    ext_sc[pl.ds(halo, tc), :] = cur_ref[...]
    ext_sc[pl.ds(halo + tc, 8), :] = jnp.zeros((8, ch), _f32)
    lead = halo - (width - 1)
    for cb in range(ch // cblk):
        cs = pl.ds(cb * cblk, cblk)
        for r in range(8):
            sh_sc[r] = ext_sc[pl.ds(r, halo + tc), cs]
        for j in range(width):
            wb_sc[j] = jnp.broadcast_to(w_ref[pl.ds(j, 1), cs], (8, cblk))
        bias = jnp.broadcast_to(bdw_ref[:, cs], (8, cblk))

        def rows(t, _):
            t0 = pl.multiple_of(t * rb, rb)
            accs = [bias] * (rb // 8)
            for j in range(width):
                off = lead + j
                wv = wb_sc[j]
                for a in range(rb // 8):
                    x = sh_sc[off % 8, pl.ds(t0 + a * 8 + (off // 8) * 8, 8), :]
                    accs[a] = accs[a] + x * wv
            for a in range(rb // 8):
                c_sc[pl.ds(t0 + a * 8, 8), cs] = accs[a]
            return 0

        lax.fori_loop(0, tc // rb, rows, 0)
    o_ref[...] = _ln_swish(c_sc[...], g_ref[...], b_ref[...]).astype(_bf16)


def _conv_prompt(glu, tp, w_dw, b_dw, ln_g, ln_b):
    ch = glu.shape[1]
    width = w_dw.shape[0]
    tc = Q_TILE
    halo = 32
    assert width - 1 <= halo and tp % tc == 0 and tc % halo == 0
    cblk = min(512, ch)
    vec = pl.BlockSpec((1, ch), lambda i: (0, 0))
    return pl.pallas_call(
        functools.partial(_conv_prompt_kernel, width=width, tc=tc, halo=halo, cblk=cblk),
        grid=(tp // tc,),
        in_specs=[pl.BlockSpec((tc, ch), lambda i: (i, 0)),
                  pl.BlockSpec((halo, ch), lambda i: (jnp.maximum(i * (tc // halo) - 1, 0), 0)),
                  pl.BlockSpec((width, ch), lambda i: (0, 0)),
                  vec, vec, vec],
        out_specs=pl.BlockSpec((tc, ch), lambda i: (i, 0)),
        out_shape=jax.ShapeDtypeStruct((tp, ch), _bf16),
        scratch_shapes=[pltpu.VMEM((halo + tc + 8, ch), _f32),
                        pltpu.VMEM((8, halo + tc, cblk), _f32),
                        pltpu.VMEM((width, 8, cblk), _f32),
                        pltpu.VMEM((tc, ch), _f32)],
        compiler_params=_cparams(("parallel",)),
        name="conv_prompt",
    )(glu, glu, w_dw, b_dw, ln_g, ln_b)


def _conv_sample_kernel(ext_ref, w_ref, bdw_ref, g_ref, b_ref, o_ref, *, width, ds):
    ch = ext_ref.shape[2]
    acc = jnp.broadcast_to(bdw_ref[...], (ds, ch))
    for j in range(width):
        acc = acc + ext_ref[0, pl.ds(j, ds), :] * w_ref[pl.ds(j, 1), :]
    o_ref[0] = _ln_swish(acc, g_ref[...], b_ref[...]).astype(_bf16)


def _conv_sample(ext, w_dw, b_dw, ln_g, ln_b, ds):
    db, rows, ch = ext.shape
    width = w_dw.shape[0]
    vec = pl.BlockSpec((1, ch), lambda b: (0, 0))
    return pl.pallas_call(
        functools.partial(_conv_sample_kernel, width=width, ds=ds),
        grid=(db,),
        in_specs=[pl.BlockSpec((1, rows, ch), lambda b: (b, 0, 0)),
                  pl.BlockSpec((width, ch), lambda b: (0, 0)),
                  vec, vec, vec],
        out_specs=pl.BlockSpec((1, ds, ch), lambda b: (b, 0, 0)),
        out_shape=jax.ShapeDtypeStruct((db, ds, ch), _bf16),
        compiler_params=_cparams(("parallel",)),
        name="conv_sample",
    )(ext, w_dw, b_dw, ln_g, ln_b)


def _merge_kernel(ca_ref, ao_ref, wco_ref, wo_ref, bco_ref, sgc_ref, sga_ref, m_ref):
    yc = _dot(ca_ref[...], wco_ref[...]) + bco_ref[...]
    ya = _dot(ao_ref[...], wo_ref[...])
    m_ref[...] = (sgc_ref[...].astype(_f32) * yc + sga_ref[...].astype(_f32) * ya).astype(_bf16)


def _merge(ca, ao, wco, wo, bco, sg):
    mp, ch = ca.shape
    aw = ao.shape[1]
    d = wco.shape[1]
    tm = _row_tile(mp, PROJ_ROW_CAP)
    tn = min(512, d)
    nj = d // tn
    return pl.pallas_call(
        _merge_kernel,
        grid=(mp // tm, nj),
        in_specs=[pl.BlockSpec((tm, ch), lambda i, j: (i, 0)),
                  pl.BlockSpec((tm, aw), lambda i, j: (i, 0)),
                  pl.BlockSpec((ch, tn), lambda i, j: (0, j)),
                  pl.BlockSpec((aw, tn), lambda i, j: (0, j)),
                  pl.BlockSpec((1, tn), lambda i, j: (0, j)),
                  pl.BlockSpec((tm, tn), lambda i, j: (i, j)),
                  pl.BlockSpec((tm, tn), lambda i, j: (i, j + nj))],
        out_specs=pl.BlockSpec((tm, tn), lambda i, j: (i, j)),
        out_shape=jax.ShapeDtypeStruct((mp, d), _bf16),
        compiler_params=_cparams(("parallel", "parallel")),
        name="merge",
    )(ca, ao, wco, wo, bco, sg, sg)


def _resid_kernel(m_ref, w_ref, h_ref, o_ref):
    o_ref[...] = h_ref[...] + _dot(m_ref[...], w_ref[...])


def _resid_proj(m, w, h):
    mp, d = m.shape
    n = w.shape[1]
    tm, tn = _row_tile(mp, PROJ_ROW_CAP), _col_tile(n)
    return pl.pallas_call(
        _resid_kernel,
        grid=(mp // tm, n // tn),
        in_specs=[pl.BlockSpec((tm, d), lambda i, j: (i, 0)),
                  pl.BlockSpec((d, tn), lambda i, j: (0, j)),
                  pl.BlockSpec((tm, tn), lambda i, j: (i, j))],
        out_specs=pl.BlockSpec((tm, tn), lambda i, j: (i, j)),
        out_shape=jax.ShapeDtypeStruct((mp, n), _f32),
        compiler_params=_cparams(("parallel", "parallel")),
        name="resid_proj",
    )(m, w, h)


def _order_key(s):
    b = lax.bitcast_convert_type(s, jnp.int32)
    return b ^ (lax.shift_right_arithmetic(b, 31) & jnp.int32(0x7FFFFFFF))


def _bit_search(count_ge, rows, n_bits, k, lowest):
    lowest = jnp.int32(lowest)

    def body(b, t_u):
        bit = lax.shift_left(jnp.int32(1), jnp.int32(n_bits - 1) - jnp.asarray(b, jnp.int32))
        cand = t_u | bit
        cnt = count_ge(cand + lowest)
        return jnp.where(cnt >= k, cand, t_u)

    t_u = lax.fori_loop(0, n_bits, body, jnp.zeros((rows, 1), jnp.int32))
    return t_u + lowest


def _kth_key_packed(load, pk_ref, scan, rows, width, k):
    halves = width // LANES
    i16 = jnp.int16
    half_min, half_max = -(2 ** 15), 2 ** 15 - 1

    def count16(th):
        th16 = th.astype(i16)

        def one(c, acc):
            w = jnp.where(pk_ref[c] >= th16, i16(1), i16(0))
            for j in range(halves):
                acc = acc + w[:, j * LANES:(j + 1) * LANES]
            return acc
        acc = scan(one, jnp.zeros((rows, LANES), i16))
        return jnp.sum(acc.astype(_f32), axis=1, keepdims=True)

    def fill_top(c, carry):
        pk_ref[c] = lax.shift_right_arithmetic(load(c), 16).astype(i16)
        return carry
    scan(fill_top, 0)
    top = _bit_search(count16, rows, 16, k, half_min)
    above = jnp.where(top >= half_max, 0.0, count16(jnp.minimum(top + 1, half_max)))

    def fill_low(c, carry):
        x = load(c)
        low = (x & jnp.int32(0xFFFF)) + half_min
        pk_ref[c] = jnp.where(lax.shift_right_arithmetic(x, 16) == top, low, half_min).astype(i16)
        return carry
    scan(fill_low, 0)
    low = _bit_search(count16, rows, 16, k - above, half_min)
    return lax.shift_left(top, 16) | (low - half_min)


def _select_threshold(load, store, n_iters, unroll, rows, width, k, idx_bits, pk_ref=None):
    halves = width // LANES

    def scan(fn, init):
        def body(it, carry):
            for u in range(unroll):
                carry = fn(it * unroll + u, carry)
            return carry
        return lax.fori_loop(0, n_iters, body, init)

    def count(pred):
        def one(c, acc):
            m = pred(load(c), c)
            for j in range(halves):
                acc = acc + jnp.where(m[:, j * LANES:(j + 1) * LANES], 1.0, 0.0)
            return acc
        return jnp.sum(scan(one, jnp.zeros((rows, LANES), _f32)), axis=1, keepdims=True)

    kf = jnp.float32(k)
    if pk_ref is None:
        t = _bit_search(lambda th: count(lambda x, c: x >= th), rows, 32, kf, INT_MIN)
    else:
        t = _kth_key_packed(load, pk_ref, scan, rows, width, kf)
    t = jnp.maximum(t, jnp.int32(INT_MIN + 1))
    n_ge = count(lambda x, c: x >= t)

    @pl.when(jnp.max(n_ge) > kf)
    def _():
        n_gt = count(lambda x, c: x > t)
        need = kf - n_gt
        lane = lax.broadcasted_iota(jnp.int32, (rows, width), 1)

        def ties_below(j):
            return count(lambda x, c: (x == t) & ((lane + c * width) < j))

        j_cut = _bit_search(lambda j: need - ties_below(j) + (kf - 1.0), rows, idx_bits, kf, 0)
        surplus = n_ge > kf

        def rewrite(c, carry):
            x = load(c)
            drop = (x == t) & ((lane + c * width) > j_cut) & surplus
            store(c, jnp.where(drop, jnp.int32(INT_MIN), x))
            return carry
        scan(rewrite, 0)

    return t


def _prompt_attn_kernel(iq_ref, iw_ref, q_ref, ikt_ref, kt_ref, v_ref, o_ref,
                        key_sc, pk_sc, thr_sc, wb_sc, m_sc, acc_sc,
                        *, nih, n_heads, n_kv, topk, idx_bits, sub):
    i = pl.program_id(0)
    tq = q_ref.shape[0]
    group = n_heads // n_kv
    unroll = 2

    for h in range(nih):
        wb_sc[h] = jnp.broadcast_to(iw_ref[:, h:h + 1], (tq, LANES))

    def scores(c):
        rhs = ikt_ref[c]
        halves = [jnp.zeros((tq, LANES), _f32) for _ in range(tq // LANES)]
        for h in range(nih):
            d = _dot(iq_ref[:, h * LANES:(h + 1) * LANES], rhs)
            w = wb_sc[h]
            for j in range(len(halves)):
                halves[j] = halves[j] + jnp.maximum(d[:, j * LANES:(j + 1) * LANES], 0.0) * w
        return _order_key(jnp.concatenate(halves, axis=1))

    def score_body(c, _):
        key_sc[c] = scores(c)
        return 0

    lax.fori_loop(0, i, score_body, 0)
    row = lax.broadcasted_iota(jnp.int32, (tq, tq), 0)
    col = lax.broadcasted_iota(jnp.int32, (tq, tq), 1)
    key_sc[i] = jnp.where(col <= row, scores(i), jnp.int32(INT_MIN))
    key_sc[i + 1] = jnp.full((tq, tq), INT_MIN, jnp.int32)
    n_iters = lax.shift_right_logical(i + unroll, 1)

    for rb in range(tq // sub):
        rs = pl.ds(rb * sub, sub)
        t = _select_threshold(lambda c: key_sc[c, rs, :],
                              lambda c, x: key_sc.__setitem__((c, rs, slice(None)), x),
                              n_iters, unroll, sub, tq, topk, idx_bits, pk_ref=pk_sc)
        thr_sc[rs, :] = jnp.broadcast_to(t, (sub, LANES))

    thr = jnp.concatenate([thr_sc[...]] * (tq // LANES), axis=1)
    m_sc[...] = jnp.full(m_sc.shape, NEG, _f32)
    acc_sc[...] = jnp.zeros_like(acc_sc)
    ones = jnp.ones((tq, LANES), _bf16)

    def attend(c, _):
        bias = jnp.where(key_sc[c] >= thr, 0.0, NEG)
        vc = v_ref[c]

        for n in range(n_kv):
            kt = kt_ref[c, n]
            vn1 = jnp.concatenate([vc[:, n * LANES:(n + 1) * LANES], ones], axis=1)
            for g in range(group):
                h = n * group + g
                lg = _dot(q_ref[:, h * LANES:(h + 1) * LANES], kt) + bias
                m_prev = m_sc[h]
                m_new = jnp.maximum(m_prev, jnp.max(lg, axis=1, keepdims=True))
                alpha = jnp.exp2(m_prev - m_new)
                p = jnp.exp2(lg - jnp.concatenate([m_new] * (tq // LANES), axis=1))
                acc_sc[h] = jnp.concatenate([alpha, alpha], axis=1) * acc_sc[h] + _dot(p.astype(_bf16), vn1)
                m_sc[h] = m_new
        return 0

    lax.fori_loop(0, i + 1, attend, 0)
    for h in range(n_heads):
        a = acc_sc[h]
        o_ref[:, h * LANES:(h + 1) * LANES] = (a[:, :LANES] / a[:, LANES:]).astype(_bf16)


def _prompt_attention(iq, iw, q, ikt, kt, v, *, tp, nih, n_heads, n_kv, topk):
    tq = Q_TILE
    nc = tp // tq
    resident = lambda shape: pl.BlockSpec(shape, lambda i: (0,) * len(shape),
                                          pipeline_mode=pl.Buffered(1))
    idx_bits = max(1, int(np.ceil(np.log2(tp + 1))))
    sub = min(SEARCH_ROWS, tq)
    return pl.pallas_call(
        functools.partial(_prompt_attn_kernel, nih=nih, n_heads=n_heads, n_kv=n_kv, topk=topk,
                          idx_bits=idx_bits, sub=sub),
        grid=(nc,),
        in_specs=[pl.BlockSpec((tq, nih * LANES), lambda i: (i, 0)),
                  pl.BlockSpec((tq, nih), lambda i: (i, 0)),
                  pl.BlockSpec((tq, n_heads * LANES), lambda i: (i, 0)),
                  resident((nc, LANES, tq)),
                  resident((nc, n_kv, LANES, tq)),
                  resident((nc, tq, n_kv * LANES))],
        out_specs=pl.BlockSpec((tq, n_heads * LANES), lambda i: (i, 0)),
        out_shape=jax.ShapeDtypeStruct((tp, n_heads * LANES), _bf16),
        scratch_shapes=[pltpu.VMEM((nc + 1, tq, tq), jnp.int32),
                        pltpu.VMEM((nc + 1, sub, tq), jnp.int16),
                        pltpu.VMEM((tq, LANES), jnp.int32),
                        pltpu.VMEM((nih, tq, LANES), _f32),
                        pltpu.VMEM((n_heads, tq, LANES), _f32),
                        pltpu.VMEM((n_heads, tq, 2 * LANES), _f32)],
        compiler_params=_cparams(("arbitrary",)),
        name="prompt_attention",
    )(iq, iw, q, ikt, kt, v)


def _ring_slot(g):
    return g % RING_SLOTS if isinstance(g, int) else lax.rem(g, RING_SLOTS)


def _ring_copies(pt_ref, streams, sem_ref, pps, g):
    slot = _ring_slot(g)
    return [pltpu.make_async_copy(cache.at[pt_ref[g * pps + j]], buf.at[slot, j], sem_ref.at[slot])
            for j in range(pps) for cache, buf in streams]


def _ring_step(pt_ref, streams, sem_ref, pps, g, n_total):
    ahead = RING_SLOTS - 1

    def request(step):
        for cp in _ring_copies(pt_ref, streams, sem_ref, pps, step):
            cp.start()

    @pl.when(g == 0)
    def _():
        for step in range(min(ahead, n_total)):
            request(step)

    @pl.when(g + ahead < n_total)
    def _():
        request(g + ahead)

    for cp in _ring_copies(pt_ref, streams, sem_ref, pps, g):
        cp.wait()


def _sample_index_kernel(pt_ref, iq_ref, iw_ref, ikn_ref, cache_ref, key_ref, thr_ref, buf, sem,
                         *, pps, n_steps, n_total, n_pages, n_chunks, ds, nih, topk, idx_bits):
    s = pl.program_id(1)
    g = pl.program_id(0) * n_steps + s
    _ring_step(pt_ref, [(cache_ref, buf)], sem, pps, g, n_total)
    slot = _ring_slot(g)

    def index_keys(ik_rows):
        n_keys = ik_rows.shape[0]
        w = jnp.concatenate([iw_ref[0]] * (n_keys // LANES), axis=1)
        x = jnp.maximum(_dot_nt(iq_ref[0], ik_rows), 0.0) * w
        return _order_key(jnp.sum(x.reshape(nih, ds, n_keys), axis=0))

    pair = 2 if pps % 2 == 0 else 1
    for j in range(0, pps, pair):
        pages = jnp.concatenate([buf[slot, j + a].astype(_bf16) for a in range(pair)], axis=0)
        keys = index_keys(pages)
        for a in range(pair):
            key_ref[0, s * pps + j + a] = keys[:, a * LANES:(a + 1) * LANES]

    @pl.when(s == n_steps - 1)
    def _():
        qi = lax.broadcasted_iota(jnp.int32, (ds, LANES), 0)
        ki = lax.broadcasted_iota(jnp.int32, (ds, LANES), 1)
        key_ref[0, n_pages] = jnp.where(ki <= qi, index_keys(ikn_ref[0]), jnp.int32(INT_MIN))
        for c in range(n_pages + 1, n_chunks):
            key_ref[0, c] = jnp.full((ds, LANES), INT_MIN, jnp.int32)
        t = _select_threshold(lambda c: key_ref[0, c],
                              lambda c, x: key_ref.__setitem__((0, c), x),
                              n_chunks // SEARCH_UNROLL_SAMPLE, SEARCH_UNROLL_SAMPLE,
                              ds, LANES, topk, idx_bits)
        thr_ref[0] = jnp.broadcast_to(t, (ds, LANES))


def _sample_index(page_table, iq_r, iw_r, ik_new, cache_ik, *, ds, nih, topk):
    db, n_pages = page_table.shape
    pps = min(PAGES_PER_STEP_INDEX, n_pages)
    assert n_pages % pps == 0
    n_steps = n_pages // pps
    page = cache_ik.shape[1]
    n_chunks = _round_up(n_pages + 1, SEARCH_UNROLL_SAMPLE)
    rows_i = nih * ds
    idx_bits = max(1, int(np.ceil(np.log2(n_chunks * LANES + 1))))
    per_b = lambda shape: pl.BlockSpec((1,) + shape, lambda b, s, pt: (b,) + (0,) * len(shape))

    grid_spec = pltpu.PrefetchScalarGridSpec(
        num_scalar_prefetch=1,
        grid=(db, n_steps),
        in_specs=[per_b((rows_i, LANES)), per_b((rows_i, LANES)), per_b((LANES, LANES)),
                  pl.BlockSpec(memory_space=pl.ANY)],
        out_specs=(per_b((n_chunks, ds, LANES)), per_b((ds, LANES))),
        scratch_shapes=[pltpu.VMEM((RING_SLOTS, pps, page, LANES), _f32),
                        pltpu.SemaphoreType.DMA((RING_SLOTS,))],
    )
    return pl.pallas_call(
        functools.partial(_sample_index_kernel, pps=pps, n_steps=n_steps, n_total=db * n_steps, n_pages=n_pages,
                          n_chunks=n_chunks, ds=ds, nih=nih, topk=topk, idx_bits=idx_bits),
        grid_spec=grid_spec,
        out_shape=(jax.ShapeDtypeStruct((db, n_chunks, ds, LANES), jnp.int32),
                   jax.ShapeDtypeStruct((db, ds, LANES), jnp.int32)),
        compiler_params=_cparams(("arbitrary", "arbitrary")),
        name="sample_index",
    )(page_table.reshape(-1), iq_r, iw_r, ik_new, cache_ik)


def _sample_attend_kernel(pt_ref, q_ref, key_ref, thr_ref, kn_ref, vn_ref, ck_ref, cv_ref, o_ref,
                          kbuf, vbuf, sem, m_sc, l_sc, acc_sc,
                          *, pps, n_steps, n_total, n_pages, ds, n_heads, n_kv):
    s = pl.program_id(1)
    g = pl.program_id(0) * n_steps + s
    _ring_step(pt_ref, [(ck_ref, kbuf), (cv_ref, vbuf)], sem, pps, g, n_total)
    slot = _ring_slot(g)
    group = n_heads // n_kv
    rows_g = group * ds
    page = kbuf.shape[2] // n_kv

    @pl.when(s == 0)
    def _():
        m_sc[...] = jnp.full(m_sc.shape, NEG, _f32)
        l_sc[...] = jnp.zeros_like(l_sc)
        acc_sc[...] = jnp.zeros_like(acc_sc)

    def attend(chunk_ids, k_of, v_of):
        thr = thr_ref[0]
        bias = jnp.concatenate([jnp.where(key_ref[0, c] >= thr, 0.0, NEG) for c in chunk_ids], axis=1)
        bias = jnp.concatenate([bias] * group, axis=0)
        m_prev, l_prev, acc_prev = m_sc[...], l_sc[...], acc_sc[...]
        heads = [slice(n * rows_g, (n + 1) * rows_g) for n in range(n_kv)]
        lgs = [_dot_nt(q_ref[0, rs, :],
                       jnp.concatenate([k_of(j, n) for j in range(len(chunk_ids))], axis=0)) + bias
               for n, rs in enumerate(heads)]
        m_out = [jnp.maximum(m_prev[rs], jnp.max(lg, axis=1, keepdims=True)) for lg, rs in zip(lgs, heads)]
        ps = [jnp.exp2(lg - m_new[:, :1]) for lg, m_new in zip(lgs, m_out)]
        pvs = [_dot(p.astype(_bf16), jnp.concatenate([v_of(j, n) for j in range(len(chunk_ids))], axis=0))
               for n, p in enumerate(ps)]
        l_out, acc_out = [], []
        for rs, m_new, p, pv in zip(heads, m_out, ps, pvs):
            alpha = jnp.exp2(m_prev[rs] - m_new)
            l_out.append(alpha * l_prev[rs] + jnp.sum(p, axis=1, keepdims=True))
            acc_out.append(alpha * acc_prev[rs] + pv)
        m_sc[...] = jnp.concatenate(m_out, axis=0)
        l_sc[...] = jnp.concatenate(l_out, axis=0)
        acc_sc[...] = jnp.concatenate(acc_out, axis=0)

    def head_rows(buf, j, n):
        return buf[slot, j, pl.ds(n, page, stride=n_kv), :].astype(_bf16)

    attend([s * pps + j for j in range(pps)],
           lambda j, n: head_rows(kbuf, j, n), lambda j, n: head_rows(vbuf, j, n))

    @pl.when(s == n_steps - 1)
    def _():
        attend([n_pages],
               lambda j, n: kn_ref[0, :, n * LANES:(n + 1) * LANES],
               lambda j, n: vn_ref[0, :, n * LANES:(n + 1) * LANES])
        o_ref[0] = (acc_sc[...] / l_sc[...]).astype(_bf16)


def _sample_attend(page_table, q_r, keys, thr, k_new, v_new, cache_k, cache_v, *, ds, n_heads, n_kv):
    db, n_pages = page_table.shape
    pps = min(PAGES_PER_STEP_ATTEND, n_pages)
    assert n_pages % pps == 0
    n_steps = n_pages // pps
    n_chunks = keys.shape[1]
    rows_q = n_heads * ds
    kvw = n_kv * LANES
    page_rows = cache_k.shape[1]
    per_b = lambda shape: pl.BlockSpec((1,) + shape, lambda b, s, pt: (b,) + (0,) * len(shape))
    hbm = pl.BlockSpec(memory_space=pl.ANY)

    grid_spec = pltpu.PrefetchScalarGridSpec(
        num_scalar_prefetch=1,
        grid=(db, n_steps),
        in_specs=[per_b((rows_q, LANES)), per_b((n_chunks, ds, LANES)), per_b((ds, LANES)),
                  per_b((LANES, kvw)), per_b((LANES, kvw)), hbm, hbm],
        out_specs=per_b((rows_q, LANES)),
        scratch_shapes=[pltpu.VMEM((RING_SLOTS, pps, page_rows, LANES), _f32),
                        pltpu.VMEM((RING_SLOTS, pps, page_rows, LANES), _f32),
                        pltpu.SemaphoreType.DMA((RING_SLOTS,)),
                        pltpu.VMEM((rows_q, LANES), _f32),
                        pltpu.VMEM((rows_q, LANES), _f32),
                        pltpu.VMEM((rows_q, LANES), _f32)],
    )
    return pl.pallas_call(
        functools.partial(_sample_attend_kernel, pps=pps, n_steps=n_steps, n_total=db * n_steps,
                          n_pages=n_pages, ds=ds,
                          n_heads=n_heads, n_kv=n_kv),
        grid_spec=grid_spec,
        out_shape=jax.ShapeDtypeStruct((db, rows_q, LANES), _bf16),
        compiler_params=_cparams(("arbitrary", "arbitrary")),
        name="sample_attend",
    )(page_table.reshape(-1), q_r, keys, thr, k_new, v_new, cache_k, cache_v)


def _rope_tables(pos, dim):
    half = dim // 2
    inv = ROPE_THETA ** (-jnp.arange(half, dtype=_f32) / half)
    ang = pos.astype(_f32)[:, None] * inv[None, :]
    cos, sin = jnp.cos(ang), jnp.sin(ang)
    return jnp.concatenate([cos, cos], axis=1), jnp.concatenate([-sin, sin], axis=1)


def kernel(x_prompt, x_sample, cache_k, cache_v, cache_ik, state_conv, page_table, meta,
           norm1_g, ffn1_w1, ffn1_w3, ffn1_w2, norm_mix_g, w_in, w_dw, b_dw, conv_ln_g, conv_ln_b,
           w_conv_out, b_conv_out, w_o, w_out, norm2_g, ffn2_w1, ffn2_w3, ffn2_w2, final_g):
    bsz, seq, d = x_prompt.shape
    db, ds, _ = x_sample.shape
    depth, n_pool, page, n_kv, hd = cache_k.shape
    idx_dim = cache_ik.shape[-1]
    n_meta = meta.shape[0]
    width = w_dw.shape[1]
    ch = w_dw.shape[2]
    attn_w = w_o.shape[1]
    n_heads = attn_w // hd
    kv_w = n_kv * hd
    in_w = w_in.shape[2]
    nih = (in_w - 2 * ch - attn_w - 2 * kv_w - idx_dim - 2 * d) // (idx_dim + 1)
    n_pages = page_table.shape[1]
    past = n_pages * page
    assert bsz == 1 and depth == 1 and hd == LANES and idx_dim == LANES and page == LANES
    assert 2 * ch + attn_w + 2 * kv_w + nih * idx_dim + idx_dim + nih + 2 * d == in_w
    assert (n_heads * ds) % 8 == 0 and ds % 8 == 0 and ds <= LANES

    t = n_meta + seq
    tp = _round_up(t, Q_TILE)
    ms = db * ds
    mp = _round_up(tp + ms, FFN_ROW_TILE)
    topk_p = min(MAX_TOPK, seq // 4)
    topk_s = min(MAX_TOPK, (past + ds) // 4)
    assert topk_p <= Q_TILE and topk_s <= LANES * (n_pages + 1)

    h0 = jnp.concatenate([meta, x_prompt[0], jnp.zeros((tp - t, d), _f32),
                          x_sample.reshape(ms, d), jnp.zeros((mp - tp - ms, d), _f32)], axis=0)
    pos = jnp.concatenate([jnp.arange(t, dtype=jnp.int32), jnp.zeros((tp - t,), jnp.int32),
                           jnp.tile(past + jnp.arange(ds, dtype=jnp.int32), db),
                           jnp.zeros((mp - tp - ms,), jnp.int32)])
    cos, sin = _rope_tables(pos, hd)

    vec = lambda a: a.reshape(1, -1).astype(_f32)
    cast = lambda a: a.astype(_bf16)
    offs = np.cumsum([0, ch, ch, attn_w, kv_w, kv_w, nih * idx_dim, idx_dim, nih, d, d])
    wi = w_in[0]
    cols = lambda a, b: cast(wi[:, offs[a]:offs[b]])

    h1, u = _ffn(h0, vec(norm1_g[0]), cast(ffn1_w1[0]), cast(ffn1_w3[0]), cast(ffn1_w2[0]),
                 vec(norm_mix_g[0]), emit_h=True)

    glu = _proj_glu(u, cols(0, 1), cols(1, 2))
    (q,) = _proj(u, cols(2, 3), mode="rope", cos=cos, sin=sin, scale=hd ** -0.5 * LOG2E,
                 out_dtypes=(_bf16,), name="proj_q")
    k32, k16 = _proj(u, cols(3, 4), mode="rope", cos=cos, sin=sin, out_dtypes=(_f32, _bf16), name="proj_k")
    v32, v16 = _proj(u, cols(4, 5), out_dtypes=(_f32, _bf16), name="proj_v")
    (iq,) = _proj(u, cols(5, 6), mode="rope", cos=cos, sin=sin, out_dtypes=(_bf16,), name="proj_iq")
    ik32, ik16 = _proj(u, cols(6, 7), mode="rope", cos=cos, sin=sin, out_dtypes=(_f32, _bf16), name="proj_ik")
    (iw,) = _proj(u, cols(7, 8), scale=(idx_dim ** -0.5) * (nih ** -0.5), name="proj_iw")
    (sg,) = _proj(u, cols(8, 10), mode="sigmoid", out_dtypes=(_bf16,), name="proj_gates")

    ca_p = _conv_prompt(glu, tp, w_dw[0], vec(b_dw[0]), vec(conv_ln_g[0]), vec(conv_ln_b[0]))
    glu_s = glu[tp:tp + ms].reshape(db, ds, ch)
    ext_rows = _round_up(width - 1 + ds, 8)
    ext_s = jnp.concatenate([state_conv[0], glu_s,
                             jnp.zeros((db, ext_rows - (width - 1 + ds), ch), _f32)], axis=1)
    ca_s = _conv_sample(ext_s, w_dw[0], vec(b_dw[0]), vec(conv_ln_g[0]), vec(conv_ln_b[0]), ds)
    ca = jnp.concatenate([ca_p, ca_s.reshape(ms, ch), jnp.zeros((mp - tp - ms, ch), _bf16)], axis=0)

    nc = tp // Q_TILE
    ikt = ik16[:tp].reshape(nc, Q_TILE, idx_dim).transpose(0, 2, 1)
    kt = k16[:tp].reshape(nc, Q_TILE, n_kv, hd).transpose(0, 2, 3, 1)
    vch = v16[:tp].reshape(nc, Q_TILE, kv_w)
    ao_p = _prompt_attention(iq, iw, q, ikt, kt, vch, tp=tp, nih=nih, n_heads=n_heads, n_kv=n_kv,
                             topk=topk_p)

    sl = slice(tp, tp + ms)
    iq_r = iq[sl].reshape(db, ds, nih, idx_dim).transpose(0, 2, 1, 3).reshape(db, nih * ds, idx_dim)
    iw_r = jnp.broadcast_to(iw[sl].reshape(db, ds, nih).transpose(0, 2, 1).reshape(db, nih * ds, 1),
                            (db, nih * ds, LANES))
    q_r = q[sl].reshape(db, ds, n_heads, hd).transpose(0, 2, 1, 3).reshape(db, n_heads * ds, hd)
    pad_keys = lambda a: jnp.pad(a.reshape(db, ds, -1), ((0, 0), (0, LANES - ds), (0, 0)))
    keys_s, thr_s = _sample_index(page_table, iq_r, iw_r, pad_keys(ik16[sl]), cache_ik[0],
                                  ds=ds, nih=nih, topk=topk_s)
    ao_s = _sample_attend(page_table, q_r, keys_s, thr_s, pad_keys(k16[sl]), pad_keys(v16[sl]),
                          cache_k[0].reshape(n_pool, page * n_kv, hd),
                          cache_v[0].reshape(n_pool, page * n_kv, hd),
                          ds=ds, n_heads=n_heads, n_kv=n_kv)
    ao_s = ao_s.reshape(db, n_heads, ds, hd).transpose(0, 2, 1, 3).reshape(ms, attn_w)
    ao = jnp.concatenate([ao_p, ao_s, jnp.zeros((mp - tp - ms, attn_w), _bf16)], axis=0)

    m = _merge(ca, ao, cast(w_conv_out[0]), cast(w_o[0]), vec(b_conv_out[0]), sg)
    h2 = _resid_proj(m, cast(w_out[0]), h1)
    y = _ffn(h2, vec(norm2_g[0]), cast(ffn2_w1[0]), cast(ffn2_w3[0]), cast(ffn2_w2[0]),
             vec(final_g), emit_h=False)

    y_prompt = y[n_meta:t][None]
    y_sample = y[sl].reshape(db, ds, d)
    hist = width - 1
    new_conv_p = glu[t - hist:t][None, None]
    new_conv_s = jnp.concatenate([state_conv[0], glu_s], axis=1)[:, -hist:][None]
    return (y_prompt, y_sample,
            k32[:t].reshape(1, 1, t, n_kv, hd), v32[:t].reshape(1, 1, t, n_kv, hd),
            ik32[:t].reshape(1, 1, t, idx_dim), new_conv_p,
            k32[sl].reshape(1, db, ds, n_kv, hd), v32[sl].reshape(1, db, ds, n_kv, hd),
            ik32[sl].reshape(1, db, ds, idx_dim), new_conv_s)
```
